```python
import jax, jax.numpy as jnp
from jax import lax
import numpy as np

D_MODEL = 4096
BATCH = 1
SEQ = 16384
DEPTH = 2
DEC_BATCH = 8
DEC_SEQ = 64
PAST_LEN = 4096

CHUNK = 64
N_META = 16
GLA_W = 3 * D_MODEL // 4
SB_W = D_MODEL // 4
GLA_DK = 128
GLA_DV = 256
GLA_HEADS = GLA_W // GLA_DV
GLA_KW = GLA_HEADS * GLA_DK
GLA_RANK = 16
GLA_TAU = 16.0
GLA_BLOCK = 16
SB_DH = 128
SB_HEADS = SB_W // SB_DH
SB_SCALE = SB_DH ** -0.5
Q_BLOCK = 128
K_BLOCK = 128
FAR_POS = 2 ** 30
N_BRANCH = 2
D_FF = 11008
N_EXPERTS = 8
TOP_K = 2
D_FF_EXPERT = 4096
MOE_MIN_BLOCK = 16
MOE_MAX_BLOCK = 1024
EPS = 1e-6
SPLITS = (GLA_KW, GLA_KW, GLA_W, GLA_W, GLA_RANK, SB_W, SB_W, SB_W, N_BRANCH * D_MODEL)
D_IN = 2 * GLA_KW + 2 * GLA_W + GLA_RANK + 3 * SB_W + N_BRANCH * D_MODEL

kernel_name = "gla_stickbreaking_gated_hybrid_stream_step"


def rms_norm(x, g):
    xf = x.astype(jnp.float32)
    y = xf * lax.rsqrt(jnp.mean(xf * xf, axis=-1, keepdims=True) + EPS)
    return (y * g.astype(jnp.float32)).astype(x.dtype)


def split_cols(proj):
    idx = np.cumsum(SPLITS)[:-1].tolist()
    return jnp.split(proj, idx, axis=-1)


def gla_block(S, inp):
    q, k, v, lg = inp
    f32 = jnp.float32
    q, k, v = q.astype(f32), k.astype(f32), v.astype(f32)
    b = jnp.cumsum(lg.astype(f32), axis=1)
    C = q.shape[1]
    qe = q * jnp.exp(b)
    attn = jnp.einsum('bthc,bshc->bhts', qe, k * jnp.exp(-b))
    attn = jnp.where(jnp.tril(jnp.ones((C, C), bool)), attn, 0.0)
    o = jnp.einsum('bhts,bshv->bthv', attn, v) + jnp.einsum('bthc,bhcv->bthv', qe, S)
    b_last = b[:, -1]
    S_new = jnp.exp(b_last)[..., None] * S + jnp.einsum('bshc,bshv->bhcv', k * jnp.exp(b_last[:, None] - b), v)
    return S_new, o


def gla_scan(S0, q, k, v, lg):
    B, L = q.shape[:2]
    pad = (-L) % GLA_BLOCK

    def blocks(t):
        t = jnp.pad(t, ((0, 0), (0, pad), (0, 0), (0, 0)))
        return t.reshape(B, -1, GLA_BLOCK, *t.shape[2:]).swapaxes(0, 1)

    S, o = lax.scan(gla_block, S0.astype(jnp.float32), (blocks(q), blocks(k), blocks(v), blocks(lg)))
    o = o.swapaxes(0, 1).reshape(B, -1, GLA_HEADS, GLA_DV)[:, :L]
    return o, S


def sb_attend(q, k, v, q_pos, k_pos):
    B, Lk = k.shape[:2]
    Tq = q.shape[1]
    pad = (-Lk) % K_BLOCK
    if pad:
        k = jnp.pad(k, ((0, 0), (0, pad), (0, 0), (0, 0)))
        v = jnp.pad(v, ((0, 0), (0, pad), (0, 0), (0, 0)))
        k_pos = jnp.concatenate([k_pos, jnp.full((pad,), FAR_POS, k_pos.dtype)])
    nk = (Lk + pad) // K_BLOCK
    z = jnp.einsum('bthd,bshd->bhts', q * SB_SCALE, k, preferred_element_type=jnp.float32)
    visible = k_pos[None, :] < q_pos[:, None]
    sp = jnp.where(visible, jnp.maximum(z, 0.0) + jnp.log1p(jnp.exp(-jnp.abs(z))), 0.0)
    spb = sp.reshape(B, SB_HEADS, Tq, nk, K_BLOCK)
    tri = jnp.tril(jnp.ones((K_BLOCK, K_BLOCK), jnp.float32))
    strict = jnp.tril(jnp.ones((nk, nk), jnp.float32), -1)
    within = jnp.einsum('bhtms,sr->bhtmr', spb, tri)
    after = jnp.einsum('bhtm,mn->bhtn', spb.sum(-1), strict)
    tail = (within + after[..., None]).reshape(B, SB_HEADS, Tq, nk * K_BLOCK)
    w = jnp.where(visible, jnp.exp(z - tail), 0.0)
    return jnp.einsum('bhts,bshd->bthd', w.astype(v.dtype), v)


def sb_prompt(q, k, v):
    L = q.shape[1]
    pad = (-L) % Q_BLOCK
    padt = lambda t: jnp.pad(t, ((0, 0), (0, pad), (0, 0), (0, 0)))
    q, k, v = padt(q), padt(k), padt(v)
    nb = (L + pad) // Q_BLOCK
    outs = []
    for i in range(nb):
        lo, hi = i * Q_BLOCK, (i + 1) * Q_BLOCK
        outs.append(sb_attend(q[:, lo:hi], k[:, :hi], v[:, :hi], jnp.arange(lo, hi), jnp.arange(hi)))
    return jnp.concatenate(outs, axis=1)[:, :L]


def mixer_block(x, past, norm_g, w_in, w_alpha2, b_alpha, gla_g, q_g, k_g, w_branch_a, w_branch_b, w_out):
    B, L, _ = x.shape
    xn = rms_norm(x, norm_g)
    q_a, k_a, v_a, r_a, a_lr, q_b, k_b, v_b, gate = split_cols(xn @ w_in)
    heads = lambda t, d: t.reshape(B, L, -1, d)
    lg = jax.nn.log_sigmoid((a_lr @ w_alpha2 + b_alpha).astype(jnp.float32)) / GLA_TAU
    lg = lg.reshape(B, L, GLA_HEADS, GLA_DK)
    q_a = heads(q_a, GLA_DK) * (GLA_DK ** -0.5)
    k_a = heads(k_a, GLA_DK)
    v_a = heads(v_a, GLA_DV)
    if past is None:
        S0 = jnp.zeros((B, GLA_HEADS, GLA_DK, GLA_DV), jnp.float32)
    else:
        k_cache, v_cache, S0 = past
    o_a, S_new = gla_scan(S0, q_a, k_a, v_a, lg)
    o_a = rms_norm(o_a, gla_g).astype(x.dtype) * jax.nn.silu(heads(r_a, GLA_DV))
    q_b = rms_norm(heads(q_b, SB_DH), q_g)
    k_b = rms_norm(heads(k_b, SB_DH), k_g)
    v_b = heads(v_b, SB_DH)
    if past is None:
        o_b = sb_prompt(q_b, k_b, v_b)
    else:
        past_len = k_cache.shape[1]
        k_all = jnp.concatenate([k_cache.astype(k_b.dtype), k_b], axis=1)
        v_all = jnp.concatenate([v_cache.astype(v_b.dtype), v_b], axis=1)
        o_b = sb_attend(q_b, k_all, v_all, past_len + jnp.arange(L), jnp.arange(past_len + L))
    g = jax.nn.sigmoid(gate.reshape(B, L, N_BRANCH, D_MODEL))
    merged = g[:, :, 0] * (o_a.reshape(B, L, GLA_W) @ w_branch_a) + g[:, :, 1] * (o_b.reshape(B, L, SB_W) @ w_branch_b)
    return x + merged @ w_out, (k_b, v_b, S_new)


def swiglu(x, wg, wu, wd):
    return (jax.nn.silu(x @ wg) * (x @ wu)) @ wd


def moe_block_rows(n_assign):
    per_expert = max(1, n_assign // N_EXPERTS)
    return int(min(MOE_MAX_BLOCK, max(MOE_MIN_BLOCK, 1 << (per_expert - 1).bit_length())))


def moe_ffn(x, w_router, wg, wu, wd):
    B, L, D = x.shape
    xf = x.reshape(-1, D)
    N = xf.shape[0]
    logits = (xf @ w_router).astype(jnp.float32)
    top_logit, top_idx = lax.top_k(logits, TOP_K)
    top_w = jax.nn.softmax(top_logit, axis=-1)
    n_assign = N * TOP_K
    blk = moe_block_rows(n_assign)
    n_rows = (n_assign + N_EXPERTS * (blk - 1) + blk - 1) // blk * blk
    n_blocks = n_rows // blk
    e_flat = top_idx.reshape(-1)
    tok_flat = jnp.arange(n_assign) // TOP_K
    w_flat = top_w.reshape(-1)
    order = jnp.argsort(e_flat)
    e_sorted = e_flat[order]
    counts = jnp.zeros(N_EXPERTS, jnp.int32).at[e_flat].add(1)
    padded = (counts + blk - 1) // blk * blk
    start = jnp.cumsum(counts) - counts
    pend = jnp.cumsum(padded)
    pstart = pend - padded
    dest = pstart[e_sorted] + (jnp.arange(n_assign) - start[e_sorted])
    row_tok = jnp.zeros(n_rows, jnp.int32).at[dest].set(tok_flat[order])
    row_w = jnp.zeros(n_rows, jnp.float32).at[dest].set(w_flat[order])
    block_expert = jnp.clip(jnp.searchsorted(pend, jnp.arange(n_blocks) * blk, side='right'), 0, N_EXPERTS - 1)

    def expert_block(args):
        toks, e = args
        xb = xf[toks]
        return (jax.nn.silu(xb @ wg[e]) * (xb @ wu[e])) @ wd[e]

    yb = lax.map(expert_block, (row_tok.reshape(n_blocks, blk), block_expert)).reshape(n_rows, D)
    y = jnp.zeros_like(xf).at[row_tok].add(yb * row_w[:, None].astype(yb.dtype))
    return y.reshape(B, L, D)


def channel_mixer(h, l, norm_g, w_ff_gate, w_ff_up, w_ff_down, w_router, w_moe_gate, w_moe_up, w_moe_down):
    hn = rms_norm(h, norm_g)
    i = l // 2
    if l % 2 == 0:
        return h + swiglu(hn, w_ff_gate[i], w_ff_up[i], w_ff_down[i])
    return h + moe_ffn(hn, w_router[i], w_moe_gate[i], w_moe_up[i], w_moe_down[i])


def setup_inputs(seed: int = 0) -> dict:
    key = jax.random.key(seed)
    ks = jax.random.split(key, 24)
    f32 = jnp.float32
    nrm = lambda k, shape, fan: jax.random.normal(k, shape, f32) * (fan ** -0.5)
    n_dense = (DEPTH + 1) // 2
    n_moe = DEPTH // 2
    return {
        "x_prompt": jax.random.normal(ks[0], (BATCH, SEQ, D_MODEL), f32),
        "x_sample": jax.random.normal(ks[1], (DEC_BATCH, DEC_SEQ, D_MODEL), f32),
        "cache_k_sb": jax.random.normal(ks[2], (DEPTH, DEC_BATCH, PAST_LEN, SB_HEADS, SB_DH), f32),
        "cache_v_sb": jax.random.normal(ks[3], (DEPTH, DEC_BATCH, PAST_LEN, SB_HEADS, SB_DH), f32),
        "state_gla": 2.0 * jax.random.normal(ks[4], (DEPTH, DEC_BATCH, GLA_HEADS, GLA_DK, GLA_DV), f32),
        "meta_tokens": jax.random.normal(ks[5], (N_META, D_MODEL), f32),
        "norm_mix": 1.0 + 0.02 * jax.random.normal(ks[6], (DEPTH, D_MODEL), f32),
        "norm_ffn": 1.0 + 0.02 * jax.random.normal(ks[7], (DEPTH, D_MODEL), f32),
        "w_in": nrm(ks[8], (DEPTH, D_MODEL, D_IN), D_MODEL),
        "w_alpha2": nrm(ks[9], (DEPTH, GLA_RANK, GLA_KW), GLA_RANK),
        "b_alpha": 0.1 * jax.random.normal(ks[10], (DEPTH, GLA_KW), f32),
        "gla_norm": 1.0 + 0.02 * jax.random.normal(ks[11], (DEPTH, GLA_DV), f32),
        "q_norm": 1.0 + 0.02 * jax.random.normal(ks[12], (DEPTH, SB_DH), f32),
        "k_norm": 1.0 + 0.02 * jax.random.normal(ks[13], (DEPTH, SB_DH), f32),
        "w_branch_a": nrm(ks[14], (DEPTH, GLA_W, D_MODEL), GLA_W),
        "w_branch_b": nrm(ks[15], (DEPTH, SB_W, D_MODEL), SB_W),
        "w_out": nrm(ks[16], (DEPTH, D_MODEL, D_MODEL), D_MODEL),
        "w_ff_gate": nrm(ks[17], (n_dense, D_MODEL, D_FF), D_MODEL),
        "w_ff_up": nrm(ks[18], (n_dense, D_MODEL, D_FF), D_MODEL),
        "w_ff_down": nrm(ks[19], (n_dense, D_FF, D_MODEL), D_FF),
        "w_router": nrm(ks[20], (n_moe, D_MODEL, N_EXPERTS), D_MODEL),
        "w_moe_gate": nrm(ks[21], (n_moe, N_EXPERTS, D_MODEL, D_FF_EXPERT), D_MODEL),
        "w_moe_up": nrm(ks[22], (n_moe, N_EXPERTS, D_MODEL, D_FF_EXPERT), D_MODEL),
        "w_moe_down": nrm(ks[23], (n_moe, N_EXPERTS, D_FF_EXPERT, D_MODEL), D_FF_EXPERT),
    }


def reference(x_prompt, x_sample, cache_k_sb, cache_v_sb, state_gla, meta_tokens, norm_mix, norm_ffn,
              w_in, w_alpha2, b_alpha, gla_norm, q_norm, k_norm, w_branch_a, w_branch_b, w_out,
              w_ff_gate, w_ff_up, w_ff_down, w_router, w_moe_gate, w_moe_up, w_moe_down):
    B = x_prompt.shape[0]
    meta = jnp.broadcast_to(meta_tokens[None].astype(x_prompt.dtype), (B, N_META, D_MODEL))
    h_p = jnp.concatenate([meta, x_prompt], axis=1)
    h_s = x_sample
    kp, vp, sp, ks_, vs_, ss_ = [], [], [], [], [], []
    for l in range(DEPTH):
        mix_w = (norm_mix[l], w_in[l], w_alpha2[l], b_alpha[l], gla_norm[l], q_norm[l], k_norm[l],
                 w_branch_a[l], w_branch_b[l], w_out[l])
        ffn_w = (norm_ffn[l], w_ff_gate, w_ff_up, w_ff_down, w_router, w_moe_gate, w_moe_up, w_moe_down)
        h_p, (k_new, v_new, S_new) = mixer_block(h_p, None, *mix_w)
        h_p = channel_mixer(h_p, l, *ffn_w)
        kp.append(k_new)
        vp.append(v_new)
        sp.append(S_new.astype(x_prompt.dtype))
        h_s, (k_new, v_new, S_new) = mixer_block(h_s, (cache_k_sb[l], cache_v_sb[l], state_gla[l]), *mix_w)
        h_s = channel_mixer(h_s, l, *ffn_w)
        ks_.append(k_new.astype(cache_k_sb.dtype))
        vs_.append(v_new.astype(cache_v_sb.dtype))
        ss_.append(S_new.astype(state_gla.dtype))
    y_prompt = h_p[:, N_META:]
    y_sample = h_s
    k_sb_prompt = jnp.stack(kp)
    v_sb_prompt = jnp.stack(vp)
    state_gla_prompt = jnp.stack(sp)
    k_sb_sample = jnp.stack(ks_)
    v_sb_sample = jnp.stack(vs_)
    state_gla_sample = jnp.stack(ss_)
    return (y_prompt, y_sample, k_sb_prompt, v_sb_prompt, state_gla_prompt, k_sb_sample, v_sb_sample, state_gla_sample)
```

```python
import functools

import jax
import jax.numpy as jnp
from jax import lax
from jax.experimental import pallas as pl
from jax.experimental.pallas import tpu as pltpu

F32 = jnp.float32
BF16 = jnp.bfloat16

GLA_DK = 128
GLA_DV = 256
GLA_TAU = 16.0
SB_DH = 128
SB_SCALE = SB_DH ** -0.5
N_EXPERTS = 8
EPS = 1e-6

LANE = 128
V7X_VMEM_BYTES = 64 * 1024 * 1024
VMEM_LIMIT = V7X_VMEM_BYTES - 8 * 1024 * 1024

ROW_ALIGN = 128
GLA_SUB = 32
MOE_TILE = 512
D_FF_ALIGN = 1024

NT_DIMS = (((1,), (1,)), ((), ()))
TN_DIMS = (((0,), (0,)), ((), ()))


def _cparams(sem):
    return pltpu.CompilerParams(dimension_semantics=sem, vmem_limit_bytes=VMEM_LIMIT)


def _round_up(x, m):
    return (x + m - 1) // m * m


def _tile(n, cap, align=LANE):
    best = None
    t = align
    while t <= min(n, cap):
        if n % t == 0:
            best = t
        t += align
    assert best is not None, (n, cap, align)
    return best


def _dot(a, b):
    return jnp.dot(a, b, preferred_element_type=F32)


def _softplus(z):
    return jnp.maximum(z, 0.0) + jnp.log1p(jnp.exp(-jnp.abs(z)))


def _rmsnorm_kernel(x_ref, g_ref, o_ref):
    x = x_ref[...]
    ms = jnp.mean(x * x, axis=-1, keepdims=True)
    o_ref[...] = (x * lax.rsqrt(ms + EPS) * g_ref[...]).astype(o_ref.dtype)


def rmsnorm(x, g, tr):
    m, d = x.shape
    return pl.pallas_call(
        _rmsnorm_kernel,
        grid=(m // tr,),
        in_specs=[pl.BlockSpec((tr, d), lambda i: (i, 0)),
                  pl.BlockSpec((1, d), lambda i: (0, 0))],
        out_specs=pl.BlockSpec((tr, d), lambda i: (i, 0)),
        out_shape=jax.ShapeDtypeStruct((m, d), BF16),
        compiler_params=_cparams(("parallel",)),
        name="rmsnorm",
    )(x, g.reshape(1, d))


def _rmsnorm_router_kernel(x_ref, g_ref, wr_ref, o_ref, idx_ref, wgt_ref):
    x = x_ref[...]
    ms = jnp.mean(x * x, axis=-1, keepdims=True)
    xn = x * lax.rsqrt(ms + EPS) * g_ref[...]
    o_ref[...] = xn.astype(o_ref.dtype)
    logits = jnp.dot(xn, wr_ref[...], preferred_element_type=F32,
                     precision=lax.Precision.HIGHEST)
    lane = lax.broadcasted_iota(jnp.int32, logits.shape, 1).astype(F32)
    neg = jnp.float32(-jnp.inf)
    l0 = jnp.where(lane < N_EXPERTS, logits, neg)
    m0 = jnp.max(l0, axis=-1, keepdims=True)
    i0 = jnp.min(jnp.where(l0 == m0, lane, float(LANE)), axis=-1, keepdims=True)
    l1 = jnp.where(lane == i0, neg, l0)
    m1 = jnp.max(l1, axis=-1, keepdims=True)
    i1 = jnp.min(jnp.where(l1 == m1, lane, float(LANE)), axis=-1, keepdims=True)
    e1 = jnp.exp(m1 - m0)
    den = 1.0 + e1
    idx_ref[...] = jnp.where(lane == 0.0, i0, i1).astype(jnp.int32)
    wgt_ref[...] = jnp.where(lane == 0.0, 1.0 / den, e1 / den)


def rmsnorm_router(x, g, w_router, tr):
    m, d = x.shape
    wr = jnp.pad(w_router.astype(F32), ((0, 0), (0, LANE - w_router.shape[1])))
    row = lambda i: (i, 0)
    return pl.pallas_call(
        _rmsnorm_router_kernel,
        grid=(m // tr,),
        in_specs=[pl.BlockSpec((tr, d), row),
                  pl.BlockSpec((1, d), lambda i: (0, 0)),
                  pl.BlockSpec((d, LANE), lambda i: (0, 0))],
        out_specs=[pl.BlockSpec((tr, d), row),
                   pl.BlockSpec((tr, LANE), row),
                   pl.BlockSpec((tr, LANE), row)],
        out_shape=[jax.ShapeDtypeStruct((m, d), BF16),
                   jax.ShapeDtypeStruct((m, LANE), jnp.int32),
                   jax.ShapeDtypeStruct((m, LANE), F32)],
        compiler_params=_cparams(("parallel",)),
        name="rmsnorm_router",
    )(x, g.reshape(1, d), wr)


def _mm_kernel(x_ref, w_ref, o_ref):
    o_ref[...] = _dot(x_ref[...], w_ref[...]).astype(o_ref.dtype)


def _mm_res_kernel(x_ref, w_ref, r_ref, o_ref):
    o_ref[...] = (r_ref[...] + _dot(x_ref[...], w_ref[...])).astype(o_ref.dtype)


def matmul(x, w, tm, tn, out_dtype, res=None, name="matmul"):
    m, k = x.shape
    n = w.shape[1]
    in_specs = [pl.BlockSpec((tm, k), lambda i, j: (i, 0)),
                pl.BlockSpec((k, tn), lambda i, j: (0, j))]
    args = [x, w]
    kern = _mm_kernel
    if res is not None:
        in_specs.append(pl.BlockSpec((tm, tn), lambda i, j: (i, j)))
        args.append(res)
        kern = _mm_res_kernel
    return pl.pallas_call(
        kern,
        grid=(m // tm, n // tn),
        in_specs=in_specs,
        out_specs=pl.BlockSpec((tm, tn), lambda i, j: (i, j)),
        out_shape=jax.ShapeDtypeStruct((m, n), out_dtype),
        compiler_params=_cparams(("parallel", "arbitrary")),
        name=name,
    )(*args)


def _mm_acc_res_kernel(x_ref, w_ref, r_ref, o_ref, acc_ref):
    kk = pl.program_id(2)

    @pl.when(kk == 0)
    def _():
        acc_ref[...] = jnp.zeros_like(acc_ref)

    acc_ref[...] += _dot(x_ref[...], w_ref[...])

    @pl.when(kk == pl.num_programs(2) - 1)
    def _():
        o_ref[...] = r_ref[...] + acc_ref[...]


def matmul_acc_res(x, w, res, tm, tn, tk, name="matmul_acc"):
    m, k = x.shape
    n = w.shape[1]
    return pl.pallas_call(
        _mm_acc_res_kernel,
        grid=(m // tm, n // tn, k // tk),
        in_specs=[pl.BlockSpec((tm, tk), lambda i, j, kk: (i, kk)),
                  pl.BlockSpec((tk, tn), lambda i, j, kk: (kk, j)),
                  pl.BlockSpec((tm, tn), lambda i, j, kk: (i, j))],
        out_specs=pl.BlockSpec((tm, tn), lambda i, j, kk: (i, j)),
        out_shape=jax.ShapeDtypeStruct((m, n), F32),
        scratch_shapes=[pltpu.VMEM((tm, tn), F32)],
        compiler_params=_cparams(("parallel", "arbitrary", "arbitrary")),
        name=name,
    )(x, w, res)


def _swiglu_kernel(x_ref, wg_ref, wu_ref, o_ref):
    x = x_ref[...]
    a = _dot(x, wg_ref[...])
    u = _dot(x, wu_ref[...])
    o_ref[...] = (a * jax.nn.sigmoid(a) * u).astype(o_ref.dtype)


def swiglu_up(x, wg, wu, tm, tn):
    m, k = x.shape
    n = wg.shape[1]
    wspec = pl.BlockSpec((k, tn), lambda i, j: (0, j))
    return pl.pallas_call(
        _swiglu_kernel,
        grid=(m // tm, n // tn),
        in_specs=[pl.BlockSpec((tm, k), lambda i, j: (i, 0)), wspec, wspec],
        out_specs=pl.BlockSpec((tm, tn), lambda i, j: (i, j)),
        out_shape=jax.ShapeDtypeStruct((m, n), BF16),
        compiler_params=_cparams(("parallel", "arbitrary")),
        name="swiglu_up",
    )(x, wg, wu)


def _merge_kernel(oa_ref, wa_ref, ob_ref, wb_ref, ga_ref, gb_ref, o_ref):
    a = _dot(oa_ref[...], wa_ref[...])
    b = _dot(ob_ref[...], wb_ref[...])
    o_ref[...] = (jax.nn.sigmoid(ga_ref[...]) * a + jax.nn.sigmoid(gb_ref[...]) * b).astype(o_ref.dtype)


def branch_merge(oa, wa, ob, wb, proj, gate_col, tm, tn):
    m, ka = oa.shape
    kb = ob.shape[1]
    n = wa.shape[1]
    ga0 = gate_col // tn
    gb0 = (gate_col + n) // tn
    return pl.pallas_call(
        _merge_kernel,
        grid=(m // tm, n // tn),
        in_specs=[pl.BlockSpec((tm, ka), lambda i, j: (i, 0)),
                  pl.BlockSpec((ka, tn), lambda i, j: (0, j)),
                  pl.BlockSpec((tm, kb), lambda i, j: (i, 0)),
                  pl.BlockSpec((kb, tn), lambda i, j: (0, j)),
                  pl.BlockSpec((tm, tn), lambda i, j: (i, ga0 + j)),
                  pl.BlockSpec((tm, tn), lambda i, j: (i, gb0 + j))],
        out_specs=pl.BlockSpec((tm, tn), lambda i, j: (i, j)),
        out_shape=jax.ShapeDtypeStruct((m, n), BF16),
        compiler_params=_cparams(("parallel", "arbitrary")),
        name="branch_merge",
    )(oa, wa, ob, wb, proj, proj)


def _alpha_kernel(x_ref, w1_ref, w2_ref, b_ref, o_ref):
    a = _dot(x_ref[...], w1_ref[...])
    z = _dot(a.astype(BF16), w2_ref[...]) + b_ref[...]
    o_ref[...] = (jnp.minimum(z, 0.0) - jnp.log1p(jnp.exp(-jnp.abs(z)))) * (1.0 / GLA_TAU)


def gla_log_decay(xn, w1, w2, b, tr):
    m, d = xn.shape
    kw = w2.shape[1]
    return pl.pallas_call(
        _alpha_kernel,
        grid=(m // tr,),
        in_specs=[pl.BlockSpec((tr, d), lambda i: (i, 0)),
                  pl.BlockSpec((d, LANE), lambda i: (0, 0)),
                  pl.BlockSpec((LANE, kw), lambda i: (0, 0)),
                  pl.BlockSpec((1, kw), lambda i: (0, 0))],
        out_specs=pl.BlockSpec((tr, kw), lambda i: (i, 0)),
        out_shape=jax.ShapeDtypeStruct((m, kw), F32),
        compiler_params=_cparams(("parallel",)),
        name="gla_log_decay",
    )(xn, w1, w2, b.reshape(1, kw))


def _gla_kernel(q_ref, k_ref, v_ref, r_ref, lg_ref, s0_ref, g_ref, o_ref, sout_ref, s_scr,
                *, chunk, n_valid):
    c = pl.program_id(2)

    @pl.when(c == 0)
    def _():
        s_scr[...] = s0_ref[...]

    q = q_ref[...] * (GLA_DK ** -0.5)
    k = k_ref[...]
    lg = lg_ref[...]
    if n_valid is not None:
        row = c * chunk + lax.broadcasted_iota(jnp.int32, (chunk, 1), 0)
        valid = row < n_valid
        lg = jnp.where(valid, lg, 0.0)
        k = jnp.where(valid, k, 0.0)

    ri = lax.broadcasted_iota(jnp.int32, (chunk, chunk), 0)
    ci = lax.broadcasted_iota(jnp.int32, (chunk, chunk), 1)
    tril = (ci <= ri).astype(BF16)
    hi = lg.astype(BF16)
    r1 = lg - hi.astype(F32)
    mid = r1.astype(BF16)
    lo = (r1 - mid.astype(F32)).astype(BF16)
    b = _dot(tril, hi) + _dot(tril, mid) + _dot(tril, lo)

    st = s_scr[...]
    vb = v_ref[...].astype(BF16)
    o_inter = lax.dot_general((q * jnp.exp(b)).astype(BF16), st.astype(BF16), NT_DIMS,
                              preferred_element_type=F32)

    outs = []
    for i in range(chunk // GLA_SUB):
        lo_r = i * GLA_SUB
        hi_r = lo_r + GLA_SUB
        mid_r = lo_r + GLA_SUB // 2 - 1
        ref_b = b[mid_r:mid_r + 1, :]
        qs = (q[lo_r:hi_r] * jnp.exp(b[lo_r:hi_r] - ref_b)).astype(BF16)
        ks = (k[:hi_r] * jnp.exp(ref_b - b[:hi_r])).astype(BF16)
        a = lax.dot_general(qs, ks, NT_DIMS, preferred_element_type=F32)
        t_idx = lo_r + lax.broadcasted_iota(jnp.int32, (GLA_SUB, hi_r), 0)
        s_idx = lax.broadcasted_iota(jnp.int32, (GLA_SUB, hi_r), 1)
        a = jnp.where(s_idx <= t_idx, a, 0.0)
        outs.append(o_inter[lo_r:hi_r] + _dot(a.astype(BF16), vb[:hi_r]))
    o = jnp.concatenate(outs, axis=0)

    b_last = b[chunk - 1:chunk, :]
    kd = (k * jnp.exp(b_last - b)).astype(BF16)
    s_new = st * jnp.exp(b_last) + lax.dot_general(vb, kd, TN_DIMS, preferred_element_type=F32)
    s_scr[...] = s_new

    @pl.when(c == pl.num_programs(2) - 1)
    def _():
        sout_ref[...] = s_new

    ms = jnp.mean(o * o, axis=-1, keepdims=True)
    on = o * lax.rsqrt(ms + EPS) * g_ref[...]
    r = r_ref[...]
    o_ref[...] = (on * (r * jax.nn.sigmoid(r))).astype(o_ref.dtype)


def gla(proj, lg, s0, g, *, row0, n_seq, n_chunks, chunk, n_valid, n_heads, k_col, v_col, r_col):
    base = row0 // chunk
    kc, vc, rc = k_col // GLA_DK, v_col // GLA_DV, r_col // GLA_DV
    rowblk = lambda s, c: base + s * n_chunks + c
    kern = functools.partial(_gla_kernel, chunk=chunk,
                             n_valid=None if n_valid == n_chunks * chunk else n_valid)
    state_spec = pl.BlockSpec((None, None, GLA_DV, GLA_DK), lambda s, h, c: (s, h, 0, 0))
    return pl.pallas_call(
        kern,
        grid=(n_seq, n_heads, n_chunks),
        in_specs=[pl.BlockSpec((chunk, GLA_DK), lambda s, h, c: (rowblk(s, c), h)),
                  pl.BlockSpec((chunk, GLA_DK), lambda s, h, c: (rowblk(s, c), kc + h)),
                  pl.BlockSpec((chunk, GLA_DV), lambda s, h, c: (rowblk(s, c), vc + h)),
                  pl.BlockSpec((chunk, GLA_DV), lambda s, h, c: (rowblk(s, c), rc + h)),
                  pl.BlockSpec((chunk, GLA_DK), lambda s, h, c: (rowblk(s, c), h)),
                  state_spec,
                  pl.BlockSpec((1, GLA_DV), lambda s, h, c: (0, 0))],
        out_specs=[pl.BlockSpec((chunk, GLA_DV), lambda s, h, c: (s * n_chunks + c, h)),
                   state_spec],
        out_shape=[jax.ShapeDtypeStruct((n_seq * n_chunks * chunk, n_heads * GLA_DV), BF16),
                   jax.ShapeDtypeStruct((n_seq, n_heads, GLA_DV, GLA_DK), F32)],
        scratch_shapes=[pltpu.VMEM((GLA_DV, GLA_DK), F32)],
        compiler_params=_cparams(("parallel", "parallel", "arbitrary")),
        name="gla_chunk",
    )(proj, proj, proj, proj, lg, s0, g.reshape(1, GLA_DV))


def _qknorm_kernel(q_ref, k_ref, v_ref, qg_ref, kg_ref, kn_ref, qh_ref, kh_ref, vh_ref, *, n_heads):
    qg = qg_ref[...]
    kg = kg_ref[...]
    for h in range(n_heads):
        sl = slice(h * SB_DH, (h + 1) * SB_DH)
        q = q_ref[:, sl]
        k = k_ref[:, sl]
        qn = q * lax.rsqrt(jnp.mean(q * q, axis=-1, keepdims=True) + EPS) * qg
        kn = k * lax.rsqrt(jnp.mean(k * k, axis=-1, keepdims=True) + EPS) * kg
        kn_ref[:, sl] = kn
        qh_ref[:, sl] = (qn * SB_SCALE).astype(BF16)
        kh_ref[:, sl] = kn.astype(BF16)
    vh_ref[...] = v_ref[...].astype(BF16)


def qk_norm(proj, qg, kg, q_col, sb_w, tr):
    m = proj.shape[0]
    c0 = q_col // sb_w
    row = lambda i: (i, 0)
    out = jax.ShapeDtypeStruct((m, sb_w), BF16)
    return pl.pallas_call(
        functools.partial(_qknorm_kernel, n_heads=sb_w // SB_DH),
        grid=(m // tr,),
        in_specs=[pl.BlockSpec((tr, sb_w), lambda i: (i, c0)),
                  pl.BlockSpec((tr, sb_w), lambda i: (i, c0 + 1)),
                  pl.BlockSpec((tr, sb_w), lambda i: (i, c0 + 2)),
                  pl.BlockSpec((1, SB_DH), lambda i: (0, 0)),
                  pl.BlockSpec((1, SB_DH), lambda i: (0, 0))],
        out_specs=[pl.BlockSpec((tr, sb_w), row)] * 4,
        out_shape=[jax.ShapeDtypeStruct((m, sb_w), F32), out, out, out],
        compiler_params=_cparams(("parallel",)),
        name="qk_norm",
    )(proj, proj, proj, qg.reshape(1, SB_DH), kg.reshape(1, SB_DH))


def _sb_block(q, kblk, vblk, trir, carry, vis):
    acc, rest = carry
    z = lax.dot_general(q, kblk, NT_DIMS, preferred_element_type=F32)
    sp = _softplus(z)
    if vis is not None:
        sp = jnp.where(vis, sp, 0.0)
    hi = sp.astype(BF16)
    lo = (sp - hi.astype(F32)).astype(BF16)
    cs = _dot(hi, trir) + _dot(lo, trir)
    w = jnp.exp(z - cs - rest)
    if vis is not None:
        w = jnp.where(vis, w, 0.0)
    acc = acc + _dot(w.astype(BF16), vblk)
    return acc, rest + cs[:, 0:1]


def _rev_tri(n):
    s = lax.broadcasted_iota(jnp.int32, (n, n), 0)
    r = lax.broadcasted_iota(jnp.int32, (n, n), 1)
    return (s >= r).astype(BF16)


def _strict_causal(tq, tk):
    t = lax.broadcasted_iota(jnp.int32, (tq, tk), 0)
    s = lax.broadcasted_iota(jnp.int32, (tq, tk), 1)
    return s < t


def _sb_prompt_kernel(q_ref, k_ref, v_ref, o_ref, *, blk):
    qb = pl.program_id(2)
    q = q_ref[...]
    trir = _rev_tri(blk)

    def fold(kb, carry, vis):
        off = pl.multiple_of(kb * blk, blk)
        return _sb_block(q, k_ref[pl.ds(off, blk), :], v_ref[pl.ds(off, blk), :], trir, carry, vis)

    carry = (jnp.zeros((blk, SB_DH), F32), jnp.zeros((blk, 1), F32))
    carry = fold(qb, carry, _strict_causal(blk, blk))
    carry = lax.fori_loop(0, qb, lambda it, cr: fold(qb - 1 - it, cr, None), carry)
    o_ref[...] = carry[0].astype(o_ref.dtype)


def sb_prompt(qh, kh, vh, *, n_seq, seq_rows, n_heads):
    blk = LANE
    nqb = seq_rows // blk
    kv_spec = pl.BlockSpec((seq_rows, SB_DH), lambda s, h, i: (s, h))
    return pl.pallas_call(
        functools.partial(_sb_prompt_kernel, blk=blk),
        grid=(n_seq, n_heads, nqb),
        in_specs=[pl.BlockSpec((blk, SB_DH), lambda s, h, i: (s * nqb + i, h)), kv_spec, kv_spec],
        out_specs=pl.BlockSpec((blk, SB_DH), lambda s, h, i: (s * nqb + i, h)),
        out_shape=jax.ShapeDtypeStruct((n_seq * seq_rows, n_heads * SB_DH), BF16),
        compiler_params=_cparams(("parallel", "parallel", "arbitrary")),
        name="sb_prompt",
    )(qh, kh, vh)


def _sb_sample_kernel(q_ref, kn_ref, vn_ref, kc_ref, vc_ref, o_ref, *, t_new, past, blk):
    q = q_ref[...]
    carry = (jnp.zeros((t_new, SB_DH), F32), jnp.zeros((t_new, 1), F32))
    carry = _sb_block(q, kn_ref[...], vn_ref[...], _rev_tri(t_new), carry, _strict_causal(t_new, t_new))
    trir = _rev_tri(blk)
    nb = past // blk

    def fold(it, cr):
        off = pl.multiple_of((nb - 1 - it) * blk, blk)
        kblk = kc_ref[pl.ds(off, blk), :].astype(BF16)
        vblk = vc_ref[pl.ds(off, blk), :].astype(BF16)
        return _sb_block(q, kblk, vblk, trir, cr, None)

    carry = lax.fori_loop(0, nb, fold, carry)
    o_ref[...] = carry[0].astype(o_ref.dtype)


def sb_sample(qh, kh, vh, cache_k, cache_v, layer, *, row0, n_seq, t_new, n_heads):
    past = cache_k.shape[2]
    blk = LANE
    assert past % blk == 0 and row0 % t_new == 0
    base = row0 // t_new
    new_spec = pl.BlockSpec((t_new, SB_DH), lambda s, h: (base + s, h))
    cache_spec = pl.BlockSpec((None, None, past, SB_DH), lambda s, h: (layer, s, 0, h))
    return pl.pallas_call(
        functools.partial(_sb_sample_kernel, t_new=t_new, past=past, blk=blk),
        grid=(n_seq, n_heads),
        in_specs=[new_spec, new_spec, new_spec, cache_spec, cache_spec],
        out_specs=pl.BlockSpec((t_new, SB_DH), lambda s, h: (s, h)),
        out_shape=jax.ShapeDtypeStruct((n_seq * t_new, n_heads * SB_DH), BF16),
        compiler_params=_cparams(("parallel", "parallel")),
        name="sb_sample",
    )(qh, kh, vh, cache_k, cache_v)


def _moe_up_kernel(te_ref, tv_ref, x_ref, wg_ref, wu_ref, o_ref):
    i = pl.program_id(0)

    @pl.when(tv_ref[i] != 0)
    def _():
        x = x_ref[...]
        a = _dot(x, wg_ref[...])
        u = _dot(x, wu_ref[...])
        o_ref[...] = (a * jax.nn.sigmoid(a) * u).astype(o_ref.dtype)

    @pl.when(tv_ref[i] == 0)
    def _():
        o_ref[...] = jnp.zeros_like(o_ref)


def _moe_down_kernel(te_ref, tv_ref, x_ref, w_ref, o_ref):
    i = pl.program_id(0)

    @pl.when(tv_ref[i] != 0)
    def _():
        o_ref[...] = _dot(x_ref[...], w_ref[...])

    @pl.when(tv_ref[i] == 0)
    def _():
        o_ref[...] = jnp.zeros_like(o_ref)


def _moe_wspec(k, tn, nj):
    def idx(i, j, te, tv):
        return te[i], 0, jnp.where(tv[i] != 0, j, nj - 1)
    return pl.BlockSpec((None, k, tn), idx)


def moe_up(xg, wg, wu, tile_expert, tile_valid, tm, tn):
    n_rows, k = xg.shape
    n = wg.shape[2]
    nj = n // tn
    wspec = _moe_wspec(k, tn, nj)
    return pl.pallas_call(
        _moe_up_kernel,
        grid_spec=pltpu.PrefetchScalarGridSpec(
            num_scalar_prefetch=2,
            grid=(n_rows // tm, nj),
            in_specs=[pl.BlockSpec((tm, k), lambda i, j, te, tv: (i, 0)), wspec, wspec],
            out_specs=pl.BlockSpec((tm, tn), lambda i, j, te, tv: (i, j))),
        out_shape=jax.ShapeDtypeStruct((n_rows, n), BF16),
        compiler_params=_cparams(("arbitrary", "arbitrary")),
        name="moe_up",
    )(tile_expert, tile_valid, xg, wg, wu)


def moe_down(hg, wd, tile_expert, tile_valid, tm, tn):
    n_rows, k = hg.shape
    n = wd.shape[2]
    nj = n // tn
    return pl.pallas_call(
        _moe_down_kernel,
        grid_spec=pltpu.PrefetchScalarGridSpec(
            num_scalar_prefetch=2,
            grid=(n_rows // tm, nj),
            in_specs=[pl.BlockSpec((tm, k), lambda i, j, te, tv: (i, 0)), _moe_wspec(k, tn, nj)],
            out_specs=pl.BlockSpec((tm, tn), lambda i, j, te, tv: (i, j))),
        out_shape=jax.ShapeDtypeStruct((n_rows, n), F32),
        compiler_params=_cparams(("arbitrary", "arbitrary")),
        name="moe_down",
    )(tile_expert, tile_valid, hg, wd)


def moe_ffn(h, hn, top_idx, top_w, wg, wu, wd, tn):
    m = hn.shape[0]
    n_assign = 2 * m
    tm = MOE_TILE
    n_rows = _round_up(n_assign + N_EXPERTS * (tm - 1), tm)
    n_tiles = n_rows // tm
    e_flat = top_idx.reshape(-1)
    order = jnp.argsort(e_flat, stable=True)
    e_sorted = e_flat[order]
    counts = jnp.zeros(N_EXPERTS, jnp.int32).at[e_flat].add(1)
    padded = (counts + tm - 1) // tm * tm
    start = jnp.cumsum(counts) - counts
    pend = jnp.cumsum(padded)
    pstart = pend - padded
    dest = (pstart[e_sorted] + (jnp.arange(n_assign, dtype=jnp.int32) - start[e_sorted])).astype(jnp.int32)
    row_tok = jnp.zeros(n_rows, jnp.int32).at[dest].set((order // 2).astype(jnp.int32))
    pos = jnp.zeros(n_assign, jnp.int32).at[order].set(dest).reshape(m, 2)
    tile_start = jnp.arange(n_tiles, dtype=jnp.int32) * tm
    tile_valid = (tile_start < pend[-1]).astype(jnp.int32)
    last_valid = jnp.maximum(pend[-1] // tm - 1, 0)
    tile_expert = jnp.clip(jnp.searchsorted(pend, tile_start, side="right"), 0, N_EXPERTS - 1).astype(jnp.int32)
    tile_expert = jnp.where(tile_valid != 0, tile_expert, tile_expert[last_valid])

    xg = hn[row_tok]
    hg = moe_up(xg, wg, wu, tile_expert, tile_valid, tm, tn)
    yb = moe_down(hg, wd, tile_expert, tile_valid, tm, tn)
    return h + (yb[pos[:, 0]] * top_w[:, 0:1] + yb[pos[:, 1]] * top_w[:, 1:2])


def kernel(x_prompt, x_sample, cache_k_sb, cache_v_sb, state_gla, meta_tokens, norm_mix, norm_ffn, w_in, w_alpha2, b_alpha, gla_norm, q_norm, k_norm, w_branch_a, w_branch_b, w_out, w_ff_gate, w_ff_up, w_ff_down, w_router, w_moe_gate, w_moe_up, w_moe_down):
    batch, seq, d_model = x_prompt.shape
    dec_batch, dec_seq, _ = x_sample.shape
    depth = w_in.shape[0]
    n_meta = meta_tokens.shape[0]
    past = cache_k_sb.shape[2]
    gla_rank, gla_kw = w_alpha2.shape[1], w_alpha2.shape[2]
    gla_w = w_branch_a.shape[1]
    sb_w = w_branch_b.shape[1]
    gla_heads = gla_w // GLA_DV
    sb_heads = sb_w // SB_DH
    d_ff = w_ff_gate.shape[2]
    assert gla_kw == gla_heads * GLA_DK and dec_seq % GLA_SUB == 0

    alr_col = 2 * gla_kw + 2 * gla_w
    k_col, v_col, r_col = gla_kw, 2 * gla_kw, 2 * gla_kw + gla_w
    qsb_col = alr_col
    gate_col = alr_col + 3 * sb_w
    n_main = gate_col + 2 * d_model

    lp = n_meta + seq
    lpad = _round_up(lp, ROW_ALIGN)
    m_prompt = batch * lpad
    m_sample = dec_batch * dec_seq
    m_tot = m_prompt + m_sample
    tm = _tile(m_tot, 1024)
    tr = _tile(m_tot, 256)
    tn = 512

    meta = jnp.broadcast_to(meta_tokens[None].astype(F32), (batch, n_meta, d_model))
    hp = jnp.concatenate([meta, x_prompt], axis=1)
    hp = jnp.pad(hp, ((0, 0), (0, lpad - lp), (0, 0))).reshape(m_prompt, d_model)
    h = jnp.concatenate([hp, x_sample.reshape(m_sample, d_model)], axis=0)

    cache_k = cache_k_sb.reshape(depth, dec_batch, past, sb_w)
    cache_v = cache_v_sb.reshape(depth, dec_batch, past, sb_w)
    s0_prompt = jnp.zeros((batch, gla_heads, GLA_DV, GLA_DK), F32)

    d_ffp = _round_up(d_ff, D_FF_ALIGN)

    kp, vp, sp, ks_, vs_, ss_ = [], [], [], [], [], []
    for l in range(depth):
        w_main = jnp.concatenate([w_in[l][:, :alr_col], w_in[l][:, alr_col + gla_rank:]], axis=1).astype(BF16)
        w_alr = jnp.pad(w_in[l][:, alr_col:alr_col + gla_rank], ((0, 0), (0, LANE - gla_rank))).astype(BF16)
        w_a2 = jnp.pad(w_alpha2[l], ((0, LANE - gla_rank), (0, 0))).astype(BF16)

        xn = rmsnorm(h, norm_mix[l], tr)
        proj = matmul(xn, w_main, tm, tn, F32, name="in_proj")
        lg = gla_log_decay(xn, w_alr, w_a2, b_alpha[l], tm)

        gla_cols = dict(n_heads=gla_heads, k_col=k_col, v_col=v_col, r_col=r_col)
        oa_p, s_p = gla(proj, lg, s0_prompt, gla_norm[l], row0=0, n_seq=batch, n_chunks=lpad // ROW_ALIGN,
                        chunk=ROW_ALIGN, n_valid=lp, **gla_cols)
        oa_s, s_s = gla(proj, lg, jnp.swapaxes(state_gla[l], -1, -2), gla_norm[l], row0=m_prompt, n_seq=dec_batch, n_chunks=1,
                        chunk=dec_seq, n_valid=dec_seq, **gla_cols)
        oa = jnp.concatenate([oa_p, oa_s], axis=0)

        kn, qh, kh, vh = qk_norm(proj, q_norm[l], k_norm[l], qsb_col, sb_w, tr)
        ob_p = sb_prompt(qh, kh, vh, n_seq=batch, seq_rows=lpad, n_heads=sb_heads)
        ob_s = sb_sample(qh, kh, vh, cache_k, cache_v, l, row0=m_prompt, n_seq=dec_batch, t_new=dec_seq,
                         n_heads=sb_heads)
        ob = jnp.concatenate([ob_p, ob_s], axis=0)

        merged = branch_merge(oa, w_branch_a[l].astype(BF16), ob, w_branch_b[l].astype(BF16), proj, gate_col, tm, tn)
        h = matmul(merged, w_out[l].astype(BF16), tm, tn, F32, res=h, name="out_proj")

        v_f32 = proj[:, qsb_col + 2 * sb_w:qsb_col + 3 * sb_w]
        kp.append(kn[:m_prompt].reshape(batch, lpad, sb_heads, SB_DH)[:, :lp])
        vp.append(v_f32[:m_prompt].reshape(batch, lpad, sb_heads, SB_DH)[:, :lp])
        sp.append(jnp.swapaxes(s_p, -1, -2))
        ks_.append(kn[m_prompt:].reshape(dec_batch, dec_seq, sb_heads, SB_DH))
        vs_.append(v_f32[m_prompt:].reshape(dec_batch, dec_seq, sb_heads, SB_DH))
        ss_.append(jnp.swapaxes(s_s, -1, -2))

        i = l // 2
        if l % 2 == 0:
            hn = rmsnorm(h, norm_ffn[l], tr)
            pad_c = ((0, 0), (0, d_ffp - d_ff))
            wg = jnp.pad(w_ff_gate[i], pad_c).astype(BF16)
            wu = jnp.pad(w_ff_up[i], pad_c).astype(BF16)
            wd = jnp.pad(w_ff_down[i], ((0, d_ffp - d_ff), (0, 0))).astype(BF16)
            mid = swiglu_up(hn, wg, wu, tm, tn)
            h = matmul_acc_res(mid, wd, h, tm, _tile(d_model, 1024), _tile(d_ffp, 3072), name="ffn_down")
        else:
            hn, top_idx, top_w = rmsnorm_router(h, norm_ffn[l], w_router[i], tr)
            h = moe_ffn(h, hn, top_idx[:, :2], top_w[:, :2], w_moe_gate[i].astype(BF16),
                        w_moe_up[i].astype(BF16), w_moe_down[i].astype(BF16), tn)

    hp = h[:m_prompt].reshape(batch, lpad, d_model)
    y_prompt = hp[:, n_meta:lp]
    y_sample = h[m_prompt:].reshape(dec_batch, dec_seq, d_model)
    return (y_prompt, y_sample, jnp.stack(kp), jnp.stack(vp), jnp.stack(sp),
            jnp.stack(ks_), jnp.stack(vs_), jnp.stack(ss_))
```

```python
import functools

import jax
import jax.numpy as jnp
from jax import lax
from jax.experimental import pallas as pl
from jax.experimental.pallas import tpu as pltpu

F32 = jnp.float32
BF16 = jnp.bfloat16

GLA_DK = 128
GLA_DV = 256
GLA_TAU = 16.0
SB_DH = 128
SB_SCALE = SB_DH ** -0.5
N_EXPERTS = 8
EPS = 1e-6

LANE = 128
V7X_VMEM_BYTES = 64 * 1024 * 1024
VMEM_LIMIT = V7X_VMEM_BYTES - 8 * 1024 * 1024

ROW_ALIGN = 128
GLA_SUB = 32
SB_TILE = 256
SB_ZERO_EXP = -110.0
MOE_TILE = 512
D_FF_ALIGN = 1024

NT_DIMS = (((1,), (1,)), ((), ()))
TN_DIMS = (((0,), (0,)), ((), ()))


def _cparams(sem):
    return pltpu.CompilerParams(dimension_semantics=sem, vmem_limit_bytes=VMEM_LIMIT)


def _round_up(x, m):
    return (x + m - 1) // m * m


def _tile(n, cap, align=LANE):
    best = None
    t = align
    while t <= min(n, cap):
        if n % t == 0:
            best = t
        t += align
    assert best is not None, (n, cap, align)
    return best


def _dot(a, b):
    return jnp.dot(a, b, preferred_element_type=F32)


def _softplus(z):
    return jnp.maximum(z, 0.0) + jnp.log1p(jnp.exp(-jnp.abs(z)))


def _rmsnorm_kernel(x_ref, g_ref, o_ref):
    x = x_ref[...]
    ms = jnp.mean(x * x, axis=-1, keepdims=True)
    o_ref[...] = (x * lax.rsqrt(ms + EPS) * g_ref[...]).astype(o_ref.dtype)


def rmsnorm(x, g, tr):
    m, d = x.shape
    return pl.pallas_call(
        _rmsnorm_kernel,
        grid=(m // tr,),
        in_specs=[pl.BlockSpec((tr, d), lambda i: (i, 0)),
                  pl.BlockSpec((1, d), lambda i: (0, 0))],
        out_specs=pl.BlockSpec((tr, d), lambda i: (i, 0)),
        out_shape=jax.ShapeDtypeStruct((m, d), BF16),
        compiler_params=_cparams(("parallel",)),
        name="rmsnorm",
    )(x, g.reshape(1, d))


def _rmsnorm_router_kernel(x_ref, g_ref, wr_ref, o_ref, idx_ref, wgt_ref):
    x = x_ref[...]
    ms = jnp.mean(x * x, axis=-1, keepdims=True)
    xn = x * lax.rsqrt(ms + EPS) * g_ref[...]
    o_ref[...] = xn.astype(o_ref.dtype)
    logits = jnp.dot(xn, wr_ref[...], preferred_element_type=F32,
                     precision=lax.Precision.HIGHEST)
    lane = lax.broadcasted_iota(jnp.int32, logits.shape, 1).astype(F32)
    neg = jnp.float32(-jnp.inf)
    l0 = jnp.where(lane < N_EXPERTS, logits, neg)
    m0 = jnp.max(l0, axis=-1, keepdims=True)
    i0 = jnp.min(jnp.where(l0 == m0, lane, float(LANE)), axis=-1, keepdims=True)
    l1 = jnp.where(lane == i0, neg, l0)
    m1 = jnp.max(l1, axis=-1, keepdims=True)
    i1 = jnp.min(jnp.where(l1 == m1, lane, float(LANE)), axis=-1, keepdims=True)
    e1 = jnp.exp(m1 - m0)
    den = 1.0 + e1
    idx_ref[...] = jnp.where(lane == 0.0, i0, i1).astype(jnp.int32)
    wgt_ref[...] = jnp.where(lane == 0.0, 1.0 / den, e1 / den)


def rmsnorm_router(x, g, w_router, tr):
    m, d = x.shape
    wr = jnp.pad(w_router.astype(F32), ((0, 0), (0, LANE - w_router.shape[1])))
    row = lambda i: (i, 0)
    return pl.pallas_call(
        _rmsnorm_router_kernel,
        grid=(m // tr,),
        in_specs=[pl.BlockSpec((tr, d), row),
                  pl.BlockSpec((1, d), lambda i: (0, 0)),
                  pl.BlockSpec((d, LANE), lambda i: (0, 0))],
        out_specs=[pl.BlockSpec((tr, d), row),
                   pl.BlockSpec((tr, LANE), row),
                   pl.BlockSpec((tr, LANE), row)],
        out_shape=[jax.ShapeDtypeStruct((m, d), BF16),
                   jax.ShapeDtypeStruct((m, LANE), jnp.int32),
                   jax.ShapeDtypeStruct((m, LANE), F32)],
        compiler_params=_cparams(("parallel",)),
        name="rmsnorm_router",
    )(x, g.reshape(1, d), wr)


def _mm_kernel(x_ref, w_ref, o_ref):
    o_ref[...] = _dot(x_ref[...], w_ref[...]).astype(o_ref.dtype)


def _mm_res_kernel(x_ref, w_ref, r_ref, o_ref):
    o_ref[...] = (r_ref[...] + _dot(x_ref[...], w_ref[...])).astype(o_ref.dtype)


def matmul(x, w, tm, tn, out_dtype, res=None, name="matmul"):
    m, k = x.shape
    n = w.shape[1]
    in_specs = [pl.BlockSpec((tm, k), lambda i, j: (i, 0)),
                pl.BlockSpec((k, tn), lambda i, j: (0, j))]
    args = [x, w]
    kern = _mm_kernel
    if res is not None:
        in_specs.append(pl.BlockSpec((tm, tn), lambda i, j: (i, j)))
        args.append(res)
        kern = _mm_res_kernel
    return pl.pallas_call(
        kern,
        grid=(m // tm, n // tn),
        in_specs=in_specs,
        out_specs=pl.BlockSpec((tm, tn), lambda i, j: (i, j)),
        out_shape=jax.ShapeDtypeStruct((m, n), out_dtype),
        compiler_params=_cparams(("parallel", "arbitrary")),
        name=name,
    )(*args)


def _mm_acc_res_kernel(x_ref, w_ref, r_ref, o_ref, acc_ref):
    kk = pl.program_id(2)

    @pl.when(kk == 0)
    def _():
        acc_ref[...] = jnp.zeros_like(acc_ref)

    acc_ref[...] += _dot(x_ref[...], w_ref[...])

    @pl.when(kk == pl.num_programs(2) - 1)
    def _():
        o_ref[...] = r_ref[...] + acc_ref[...]


def matmul_acc_res(x, w, res, tm, tn, tk, name="matmul_acc"):
    m, k = x.shape
    n = w.shape[1]
    return pl.pallas_call(
        _mm_acc_res_kernel,
        grid=(m // tm, n // tn, k // tk),
        in_specs=[pl.BlockSpec((tm, tk), lambda i, j, kk: (i, kk)),
                  pl.BlockSpec((tk, tn), lambda i, j, kk: (kk, j)),
                  pl.BlockSpec((tm, tn), lambda i, j, kk: (i, j))],
        out_specs=pl.BlockSpec((tm, tn), lambda i, j, kk: (i, j)),
        out_shape=jax.ShapeDtypeStruct((m, n), F32),
        scratch_shapes=[pltpu.VMEM((tm, tn), F32)],
        compiler_params=_cparams(("parallel", "arbitrary", "arbitrary")),
        name=name,
    )(x, w, res)


def _swiglu_kernel(x_ref, wg_ref, wu_ref, o_ref):
    x = x_ref[...]
    a = _dot(x, wg_ref[...])
    u = _dot(x, wu_ref[...])
    o_ref[...] = (a * jax.nn.sigmoid(a) * u).astype(o_ref.dtype)


def swiglu_up(x, wg, wu, tm, tn):
    m, k = x.shape
    n = wg.shape[1]
    wspec = pl.BlockSpec((k, tn), lambda i, j: (0, j))
    return pl.pallas_call(
        _swiglu_kernel,
        grid=(m // tm, n // tn),
        in_specs=[pl.BlockSpec((tm, k), lambda i, j: (i, 0)), wspec, wspec],
        out_specs=pl.BlockSpec((tm, tn), lambda i, j: (i, j)),
        out_shape=jax.ShapeDtypeStruct((m, n), BF16),
        compiler_params=_cparams(("parallel", "arbitrary")),
        name="swiglu_up",
    )(x, wg, wu)


def _merge_kernel(oa_ref, wa_ref, ob_ref, wb_ref, ga_ref, gb_ref, o_ref):
    a = _dot(oa_ref[...], wa_ref[...])
    b = _dot(ob_ref[...], wb_ref[...])
    o_ref[...] = (jax.nn.sigmoid(ga_ref[...]) * a + jax.nn.sigmoid(gb_ref[...]) * b).astype(o_ref.dtype)


def branch_merge(oa, wa, ob, wb, proj, gate_col, tm, tn):
    m, ka = oa.shape
    kb = ob.shape[1]
    n = wa.shape[1]
    ga0 = gate_col // tn
    gb0 = (gate_col + n) // tn
    return pl.pallas_call(
        _merge_kernel,
        grid=(m // tm, n // tn),
        in_specs=[pl.BlockSpec((tm, ka), lambda i, j: (i, 0)),
                  pl.BlockSpec((ka, tn), lambda i, j: (0, j)),
                  pl.BlockSpec((tm, kb), lambda i, j: (i, 0)),
                  pl.BlockSpec((kb, tn), lambda i, j: (0, j)),
                  pl.BlockSpec((tm, tn), lambda i, j: (i, ga0 + j)),
                  pl.BlockSpec((tm, tn), lambda i, j: (i, gb0 + j))],
        out_specs=pl.BlockSpec((tm, tn), lambda i, j: (i, j)),
        out_shape=jax.ShapeDtypeStruct((m, n), BF16),
        compiler_params=_cparams(("parallel", "arbitrary")),
        name="branch_merge",
    )(oa, wa, ob, wb, proj, proj)


def _alpha_kernel(x_ref, w1_ref, w2_ref, b_ref, o_ref):
    a = _dot(x_ref[...], w1_ref[...])
    z = _dot(a.astype(BF16), w2_ref[...]) + b_ref[...]
    o_ref[...] = (jnp.minimum(z, 0.0) - jnp.log1p(jnp.exp(-jnp.abs(z)))) * (1.0 / GLA_TAU)


def gla_log_decay(xn, w1, w2, b, tr):
    m, d = xn.shape
    kw = w2.shape[1]
    return pl.pallas_call(
        _alpha_kernel,
        grid=(m // tr,),
        in_specs=[pl.BlockSpec((tr, d), lambda i: (i, 0)),
                  pl.BlockSpec((d, LANE), lambda i: (0, 0)),
                  pl.BlockSpec((LANE, kw), lambda i: (0, 0)),
                  pl.BlockSpec((1, kw), lambda i: (0, 0))],
        out_specs=pl.BlockSpec((tr, kw), lambda i: (i, 0)),
        out_shape=jax.ShapeDtypeStruct((m, kw), F32),
        compiler_params=_cparams(("parallel",)),
        name="gla_log_decay",
    )(xn, w1, w2, b.reshape(1, kw))


def _gla_kernel(q_ref, k_ref, v_ref, r_ref, lg_ref, s0_ref, g_ref, o_ref, sout_ref, s_scr,
                *, chunk, n_valid):
    c = pl.program_id(2)

    @pl.when(c == 0)
    def _():
        s_scr[...] = s0_ref[...]

    q = q_ref[...] * (GLA_DK ** -0.5)
    k = k_ref[...]
    lg = lg_ref[...]
    if n_valid is not None:
        row = c * chunk + lax.broadcasted_iota(jnp.int32, (chunk, 1), 0)
        valid = row < n_valid
        lg = jnp.where(valid, lg, 0.0)
        k = jnp.where(valid, k, 0.0)

    ri = lax.broadcasted_iota(jnp.int32, (chunk, chunk), 0)
    ci = lax.broadcasted_iota(jnp.int32, (chunk, chunk), 1)
    tril = (ci <= ri).astype(BF16)
    hi = lg.astype(BF16)
    r1 = lg - hi.astype(F32)
    mid = r1.astype(BF16)
    lo = (r1 - mid.astype(F32)).astype(BF16)
    b = _dot(tril, hi) + _dot(tril, mid) + _dot(tril, lo)

    st = s_scr[...]
    vb = v_ref[...].astype(BF16)
    o_inter = lax.dot_general((q * jnp.exp(b)).astype(BF16), st.astype(BF16), NT_DIMS,
                              preferred_element_type=F32)

    outs = []
    for i in range(chunk // GLA_SUB):
        lo_r = i * GLA_SUB
        hi_r = lo_r + GLA_SUB
        mid_r = lo_r + GLA_SUB // 2 - 1
        ref_b = b[mid_r:mid_r + 1, :]
        qs = (q[lo_r:hi_r] * jnp.exp(b[lo_r:hi_r] - ref_b)).astype(BF16)
        ks = (k[:hi_r] * jnp.exp(ref_b - b[:hi_r])).astype(BF16)
        a = lax.dot_general(qs, ks, NT_DIMS, preferred_element_type=F32)
        t_idx = lo_r + lax.broadcasted_iota(jnp.int32, (GLA_SUB, hi_r), 0)
        s_idx = lax.broadcasted_iota(jnp.int32, (GLA_SUB, hi_r), 1)
        a = jnp.where(s_idx <= t_idx, a, 0.0)
        outs.append(o_inter[lo_r:hi_r] + _dot(a.astype(BF16), vb[:hi_r]))
    o = jnp.concatenate(outs, axis=0)

    b_last = b[chunk - 1:chunk, :]
    kd = (k * jnp.exp(b_last - b)).astype(BF16)
    s_new = st * jnp.exp(b_last) + lax.dot_general(vb, kd, TN_DIMS, preferred_element_type=F32)
    s_scr[...] = s_new

    @pl.when(c == pl.num_programs(2) - 1)
    def _():
        sout_ref[...] = s_new

    ms = jnp.mean(o * o, axis=-1, keepdims=True)
    on = o * lax.rsqrt(ms + EPS) * g_ref[...]
    r = r_ref[...]
    o_ref[...] = (on * (r * jax.nn.sigmoid(r))).astype(o_ref.dtype)


def gla(proj, lg, s0, g, *, row0, n_seq, n_chunks, chunk, n_valid, n_heads, k_col, v_col, r_col):
    base = row0 // chunk
    kc, vc, rc = k_col // GLA_DK, v_col // GLA_DV, r_col // GLA_DV
    rowblk = lambda s, c: base + s * n_chunks + c
    kern = functools.partial(_gla_kernel, chunk=chunk,
                             n_valid=None if n_valid == n_chunks * chunk else n_valid)
    state_spec = pl.BlockSpec((None, None, GLA_DV, GLA_DK), lambda s, h, c: (s, h, 0, 0))
    return pl.pallas_call(
        kern,
        grid=(n_seq, n_heads, n_chunks),
        in_specs=[pl.BlockSpec((chunk, GLA_DK), lambda s, h, c: (rowblk(s, c), h)),
                  pl.BlockSpec((chunk, GLA_DK), lambda s, h, c: (rowblk(s, c), kc + h)),
                  pl.BlockSpec((chunk, GLA_DV), lambda s, h, c: (rowblk(s, c), vc + h)),
                  pl.BlockSpec((chunk, GLA_DV), lambda s, h, c: (rowblk(s, c), rc + h)),
                  pl.BlockSpec((chunk, GLA_DK), lambda s, h, c: (rowblk(s, c), h)),
                  state_spec,
                  pl.BlockSpec((1, GLA_DV), lambda s, h, c: (0, 0))],
        out_specs=[pl.BlockSpec((chunk, GLA_DV), lambda s, h, c: (s * n_chunks + c, h)),
                   state_spec],
        out_shape=[jax.ShapeDtypeStruct((n_seq * n_chunks * chunk, n_heads * GLA_DV), BF16),
                   jax.ShapeDtypeStruct((n_seq, n_heads, GLA_DV, GLA_DK), F32)],
        scratch_shapes=[pltpu.VMEM((GLA_DV, GLA_DK), F32)],
        compiler_params=_cparams(("parallel", "parallel", "arbitrary")),
        name="gla_chunk",
    )(proj, proj, proj, proj, lg, s0, g.reshape(1, GLA_DV))


def _qknorm_kernel(q_ref, k_ref, v_ref, qg_ref, kg_ref, kn_ref, qh_ref, kh_ref, vh_ref, *, n_heads):
    qg = qg_ref[...]
    kg = kg_ref[...]
    for h in range(n_heads):
        sl = slice(h * SB_DH, (h + 1) * SB_DH)
        q = q_ref[:, sl]
        k = k_ref[:, sl]
        qn = q * lax.rsqrt(jnp.mean(q * q, axis=-1, keepdims=True) + EPS) * qg
        kn = k * lax.rsqrt(jnp.mean(k * k, axis=-1, keepdims=True) + EPS) * kg
        kn_ref[:, sl] = kn
        qh_ref[:, sl] = (qn * SB_SCALE).astype(BF16)
        kh_ref[:, sl] = kn.astype(BF16)
    vh_ref[...] = v_ref[...].astype(BF16)


def qk_norm(proj, qg, kg, q_col, sb_w, tr):
    m = proj.shape[0]
    c0 = q_col // sb_w
    row = lambda i: (i, 0)
    out = jax.ShapeDtypeStruct((m, sb_w), BF16)
    return pl.pallas_call(
        functools.partial(_qknorm_kernel, n_heads=sb_w // SB_DH),
        grid=(m // tr,),
        in_specs=[pl.BlockSpec((tr, sb_w), lambda i: (i, c0)),
                  pl.BlockSpec((tr, sb_w), lambda i: (i, c0 + 1)),
                  pl.BlockSpec((tr, sb_w), lambda i: (i, c0 + 2)),
                  pl.BlockSpec((1, SB_DH), lambda i: (0, 0)),
                  pl.BlockSpec((1, SB_DH), lambda i: (0, 0))],
        out_specs=[pl.BlockSpec((tr, sb_w), row)] * 4,
        out_shape=[jax.ShapeDtypeStruct((m, sb_w), F32), out, out, out],
        compiler_params=_cparams(("parallel",)),
        name="qk_norm",
    )(proj, proj, proj, qg.reshape(1, SB_DH), kg.reshape(1, SB_DH))


def _sb_tile(q, kblk, vblk, trir, carry, vis):
    acc, rest = carry
    z = lax.dot_general(q, kblk, NT_DIMS, preferred_element_type=F32)
    sp = _softplus(z)
    if vis is not None:
        sp = jnp.where(vis, sp, 0.0)
    hi = sp.astype(BF16)
    lo = (sp - hi.astype(F32)).astype(BF16)
    cs = _dot(hi, trir) + _dot(lo, trir)
    w = jnp.exp(z - cs - rest)
    if vis is not None:
        w = jnp.where(vis, w, 0.0)
    acc = acc + _dot(w.astype(BF16), vblk)
    return acc, rest + cs[:, 0:1]


def _rev_tri(n):
    s = lax.broadcasted_iota(jnp.int32, (n, n), 0)
    r = lax.broadcasted_iota(jnp.int32, (n, n), 1)
    return (s >= r).astype(BF16)


def _strict_causal(tq, tk):
    t = lax.broadcasted_iota(jnp.int32, (tq, tk), 0)
    s = lax.broadcasted_iota(jnp.int32, (tq, tk), 1)
    return s < t


def _key_norm_max(k_ref, rows, step):
    def body(c, m):
        off = pl.multiple_of(c * step, step)
        kf = k_ref[pl.ds(off, step), :].astype(F32)
        return jnp.maximum(m, jnp.sum(kf * kf, axis=-1, keepdims=True))

    m = lax.fori_loop(0, rows // step, body, jnp.zeros((step, 1), F32))
    return jnp.sqrt(jnp.max(m))


def _sb_sweep(q, acc, rest, n_tiles, k_norm_max, load_tile, trir):
    qf = q.astype(F32)
    q_norm_max = jnp.sqrt(jnp.max(jnp.sum(qf * qf, axis=-1, keepdims=True)))
    z_bound = q_norm_max * k_norm_max * 1.01 + 1.0

    def live(rest):
        return jnp.logical_not(z_bound - jnp.min(rest) < SB_ZERO_EXP).astype(jnp.int32)

    def cond(c):
        return jnp.logical_and(c[0] >= 0, c[3] != 0)

    def body(c):
        j, acc, rest, _ = c
        kblk, vblk = load_tile(j)
        acc, rest = _sb_tile(q, kblk, vblk, trir, (acc, rest), None)
        return j - 1, acc, rest, live(rest)

    return lax.while_loop(cond, body, (n_tiles - 1, acc, rest, live(rest)))[1]


def _sb_prompt_kernel(q_ref, k_ref, v_ref, tri_ref, o_ref, kmax_ref, *, tile, seq_rows):
    i = pl.program_id(2)

    @pl.when(i == 0)
    def _():
        kmax_ref[0] = _key_norm_max(k_ref, seq_rows, tile)

    q = q_ref[...]
    trir = tri_ref[...]

    def load(j):
        off = pl.multiple_of(j * tile, tile)
        return k_ref[pl.ds(off, tile), :], v_ref[pl.ds(off, tile), :]

    carry = (jnp.zeros((tile, SB_DH), F32), jnp.zeros((tile, 1), F32))
    kd, vd = load(i)
    acc, rest = _sb_tile(q, kd, vd, trir, carry, _strict_causal(tile, tile))
    acc = _sb_sweep(q, acc, rest, i, kmax_ref[0], load, trir)
    o_ref[...] = acc.astype(o_ref.dtype)


def sb_prompt(qh, kh, vh, tri, *, n_seq, seq_rows, n_heads):
    tile = SB_TILE
    nqt = seq_rows // tile
    kv_spec = pl.BlockSpec((seq_rows, SB_DH), lambda s, h, i: (s, h))
    q_spec = pl.BlockSpec((tile, SB_DH), lambda s, h, i: (s * nqt + i, h))
    return pl.pallas_call(
        functools.partial(_sb_prompt_kernel, tile=tile, seq_rows=seq_rows),
        grid=(n_seq, n_heads, nqt),
        in_specs=[q_spec, kv_spec, kv_spec, pl.BlockSpec((tile, tile), lambda s, h, i: (0, 0))],
        out_specs=q_spec,
        out_shape=jax.ShapeDtypeStruct((n_seq * seq_rows, n_heads * SB_DH), BF16),
        scratch_shapes=[pltpu.SMEM((1,), F32)],
        compiler_params=_cparams(("parallel", "parallel", "arbitrary")),
        name="sb_prompt",
    )(qh, kh, vh, tri)


def _sb_sample_kernel(q_ref, kn_ref, vn_ref, kc_ref, vc_ref, tri_ref, o_ref, *, t_new, past, tile):
    q = q_ref[...]
    carry = (jnp.zeros((t_new, SB_DH), F32), jnp.zeros((t_new, 1), F32))
    acc, rest = _sb_tile(q, kn_ref[...], vn_ref[...], _rev_tri(t_new), carry, _strict_causal(t_new, t_new))

    def load(j):
        off = pl.multiple_of(j * tile, tile)
        return kc_ref[pl.ds(off, tile), :].astype(BF16), vc_ref[pl.ds(off, tile), :].astype(BF16)

    acc = _sb_sweep(q, acc, rest, past // tile, _key_norm_max(kc_ref, past, tile), load, tri_ref[...])
    o_ref[...] = acc.astype(o_ref.dtype)


def sb_sample(qh, kh, vh, cache_k, cache_v, tri, layer, *, row0, n_seq, t_new, n_heads):
    past = cache_k.shape[2]
    tile = SB_TILE
    assert past % tile == 0 and row0 % t_new == 0
    base = row0 // t_new
    new_spec = pl.BlockSpec((t_new, SB_DH), lambda s, h: (base + s, h))
    cache_spec = pl.BlockSpec((None, None, past, SB_DH), lambda s, h: (layer, s, 0, h))
    return pl.pallas_call(
        functools.partial(_sb_sample_kernel, t_new=t_new, past=past, tile=tile),
        grid=(n_seq, n_heads),
        in_specs=[new_spec, new_spec, new_spec, cache_spec, cache_spec,
                  pl.BlockSpec((tile, tile), lambda s, h: (0, 0))],
        out_specs=pl.BlockSpec((t_new, SB_DH), lambda s, h: (s, h)),
        out_shape=jax.ShapeDtypeStruct((n_seq * t_new, n_heads * SB_DH), BF16),
        compiler_params=_cparams(("parallel", "parallel")),
        name="sb_sample",
    )(qh, kh, vh, cache_k, cache_v, tri)


def _moe_up_kernel(te_ref, tv_ref, x_ref, wg_ref, wu_ref, o_ref):
    i = pl.program_id(0)

    @pl.when(tv_ref[i] != 0)
    def _():
        x = x_ref[...]
        a = _dot(x, wg_ref[...])
        u = _dot(x, wu_ref[...])
        o_ref[...] = (a * jax.nn.sigmoid(a) * u).astype(o_ref.dtype)

    @pl.when(tv_ref[i] == 0)
    def _():
        o_ref[...] = jnp.zeros_like(o_ref)


def _moe_down_kernel(te_ref, tv_ref, x_ref, w_ref, o_ref):
    i = pl.program_id(0)

    @pl.when(tv_ref[i] != 0)
    def _():
        o_ref[...] = _dot(x_ref[...], w_ref[...])

    @pl.when(tv_ref[i] == 0)
    def _():
        o_ref[...] = jnp.zeros_like(o_ref)


def _moe_wspec(k, tn, nj):
    def idx(i, j, te, tv):
        return te[i], 0, jnp.where(tv[i] != 0, j, nj - 1)
    return pl.BlockSpec((None, k, tn), idx)


def moe_up(xg, wg, wu, tile_expert, tile_valid, tm, tn):
    n_rows, k = xg.shape
    n = wg.shape[2]
    nj = n // tn
    wspec = _moe_wspec(k, tn, nj)
    return pl.pallas_call(
        _moe_up_kernel,
        grid_spec=pltpu.PrefetchScalarGridSpec(
            num_scalar_prefetch=2,
            grid=(n_rows // tm, nj),
            in_specs=[pl.BlockSpec((tm, k), lambda i, j, te, tv: (i, 0)), wspec, wspec],
            out_specs=pl.BlockSpec((tm, tn), lambda i, j, te, tv: (i, j))),
        out_shape=jax.ShapeDtypeStruct((n_rows, n), BF16),
        compiler_params=_cparams(("arbitrary", "arbitrary")),
        name="moe_up",
    )(tile_expert, tile_valid, xg, wg, wu)


def moe_down(hg, wd, tile_expert, tile_valid, tm, tn):
    n_rows, k = hg.shape
    n = wd.shape[2]
    nj = n // tn
    return pl.pallas_call(
        _moe_down_kernel,
        grid_spec=pltpu.PrefetchScalarGridSpec(
            num_scalar_prefetch=2,
            grid=(n_rows // tm, nj),
            in_specs=[pl.BlockSpec((tm, k), lambda i, j, te, tv: (i, 0)), _moe_wspec(k, tn, nj)],
            out_specs=pl.BlockSpec((tm, tn), lambda i, j, te, tv: (i, j))),
        out_shape=jax.ShapeDtypeStruct((n_rows, n), F32),
        compiler_params=_cparams(("arbitrary", "arbitrary")),
        name="moe_down",
    )(tile_expert, tile_valid, hg, wd)


def moe_ffn(h, hn, top_idx, top_w, wg, wu, wd, tn):
    m = hn.shape[0]
    n_assign = 2 * m
    tm = MOE_TILE
    n_rows = _round_up(n_assign + N_EXPERTS * (tm - 1), tm)
    n_tiles = n_rows // tm
    e_flat = top_idx.reshape(-1)
    order = jnp.argsort(e_flat, stable=True)
    e_sorted = e_flat[order]
    counts = jnp.zeros(N_EXPERTS, jnp.int32).at[e_flat].add(1)
    padded = (counts + tm - 1) // tm * tm
    start = jnp.cumsum(counts) - counts
    pend = jnp.cumsum(padded)
    pstart = pend - padded
    dest = (pstart[e_sorted] + (jnp.arange(n_assign, dtype=jnp.int32) - start[e_sorted])).astype(jnp.int32)
    row_tok = jnp.zeros(n_rows, jnp.int32).at[dest].set((order // 2).astype(jnp.int32))
    pos = jnp.zeros(n_assign, jnp.int32).at[order].set(dest).reshape(m, 2)
    tile_start = jnp.arange(n_tiles, dtype=jnp.int32) * tm
    tile_valid = (tile_start < pend[-1]).astype(jnp.int32)
    last_valid = jnp.maximum(pend[-1] // tm - 1, 0)
    tile_expert = jnp.clip(jnp.searchsorted(pend, tile_start, side="right"), 0, N_EXPERTS - 1).astype(jnp.int32)
    tile_expert = jnp.where(tile_valid != 0, tile_expert, tile_expert[last_valid])

    xg = hn[row_tok]
    hg = moe_up(xg, wg, wu, tile_expert, tile_valid, tm, tn)
    yb = moe_down(hg, wd, tile_expert, tile_valid, tm, tn)
    return h + (yb[pos[:, 0]] * top_w[:, 0:1] + yb[pos[:, 1]] * top_w[:, 1:2])


def kernel(x_prompt, x_sample, cache_k_sb, cache_v_sb, state_gla, meta_tokens, norm_mix, norm_ffn, w_in, w_alpha2, b_alpha, gla_norm, q_norm, k_norm, w_branch_a, w_branch_b, w_out, w_ff_gate, w_ff_up, w_ff_down, w_router, w_moe_gate, w_moe_up, w_moe_down):
    batch, seq, d_model = x_prompt.shape
    dec_batch, dec_seq, _ = x_sample.shape
    depth = w_in.shape[0]
    n_meta = meta_tokens.shape[0]
    past = cache_k_sb.shape[2]
    gla_rank, gla_kw = w_alpha2.shape[1], w_alpha2.shape[2]
    gla_w = w_branch_a.shape[1]
    sb_w = w_branch_b.shape[1]
    gla_heads = gla_w // GLA_DV
    sb_heads = sb_w // SB_DH
    d_ff = w_ff_gate.shape[2]
    assert gla_kw == gla_heads * GLA_DK and dec_seq % GLA_SUB == 0

    alr_col = 2 * gla_kw + 2 * gla_w
    k_col, v_col, r_col = gla_kw, 2 * gla_kw, 2 * gla_kw + gla_w
    qsb_col = alr_col
    gate_col = alr_col + 3 * sb_w
    n_main = gate_col + 2 * d_model

    lp = n_meta + seq
    lpad = _round_up(lp, ROW_ALIGN)
    m_prompt = batch * lpad
    m_sample = dec_batch * dec_seq
    m_tot = m_prompt + m_sample
    tm = _tile(m_tot, 1024)
    tr = _tile(m_tot, 256)
    tn = 512

    meta = jnp.broadcast_to(meta_tokens[None].astype(F32), (batch, n_meta, d_model))
    hp = jnp.concatenate([meta, x_prompt], axis=1)
    hp = jnp.pad(hp, ((0, 0), (0, lpad - lp), (0, 0))).reshape(m_prompt, d_model)
    h = jnp.concatenate([hp, x_sample.reshape(m_sample, d_model)], axis=0)

    cache_k = cache_k_sb.reshape(depth, dec_batch, past, sb_w)
    cache_v = cache_v_sb.reshape(depth, dec_batch, past, sb_w)
    s0_prompt = jnp.zeros((batch, gla_heads, GLA_DV, GLA_DK), F32)

    d_ffp = _round_up(d_ff, D_FF_ALIGN)

    lpa = _round_up(lpad, SB_TILE)
    tri = jnp.tril(jnp.ones((SB_TILE, SB_TILE), BF16))

    def attn_rows(a):
        p = a[:m_prompt].reshape(batch, lpad, sb_w)
        return jnp.pad(p, ((0, 0), (0, lpa - lpad), (0, 0))).reshape(batch * lpa, sb_w)

    kp, vp, sp, ks_, vs_, ss_ = [], [], [], [], [], []
    for l in range(depth):
        w_main = jnp.concatenate([w_in[l][:, :alr_col], w_in[l][:, alr_col + gla_rank:]], axis=1).astype(BF16)
        w_alr = jnp.pad(w_in[l][:, alr_col:alr_col + gla_rank], ((0, 0), (0, LANE - gla_rank))).astype(BF16)
        w_a2 = jnp.pad(w_alpha2[l], ((0, LANE - gla_rank), (0, 0))).astype(BF16)

        xn = rmsnorm(h, norm_mix[l], tr)
        proj = matmul(xn, w_main, tm, tn, F32, name="in_proj")
        lg = gla_log_decay(xn, w_alr, w_a2, b_alpha[l], tm)

        gla_cols = dict(n_heads=gla_heads, k_col=k_col, v_col=v_col, r_col=r_col)
        oa_p, s_p = gla(proj, lg, s0_prompt, gla_norm[l], row0=0, n_seq=batch, n_chunks=lpad // ROW_ALIGN,
                        chunk=ROW_ALIGN, n_valid=lp, **gla_cols)
        oa_s, s_s = gla(proj, lg, jnp.swapaxes(state_gla[l], -1, -2), gla_norm[l], row0=m_prompt, n_seq=dec_batch, n_chunks=1,
                        chunk=dec_seq, n_valid=dec_seq, **gla_cols)
        oa = jnp.concatenate([oa_p, oa_s], axis=0)

        kn, qh, kh, vh = qk_norm(proj, q_norm[l], k_norm[l], qsb_col, sb_w, tr)
        ob_p = sb_prompt(attn_rows(qh), attn_rows(kh), attn_rows(vh), tri, n_seq=batch, seq_rows=lpa,
                         n_heads=sb_heads)
        ob_p = ob_p.reshape(batch, lpa, sb_w)[:, :lpad].reshape(m_prompt, sb_w)
        ob_s = sb_sample(qh, kh, vh, cache_k, cache_v, tri, l, row0=m_prompt, n_seq=dec_batch, t_new=dec_seq,
                         n_heads=sb_heads)
        ob = jnp.concatenate([ob_p, ob_s], axis=0)

        merged = branch_merge(oa, w_branch_a[l].astype(BF16), ob, w_branch_b[l].astype(BF16), proj, gate_col, tm, tn)
        h = matmul(merged, w_out[l].astype(BF16), tm, tn, F32, res=h, name="out_proj")

        v_f32 = proj[:, qsb_col + 2 * sb_w:qsb_col + 3 * sb_w]
        kp.append(kn[:m_prompt].reshape(batch, lpad, sb_heads, SB_DH)[:, :lp])
        vp.append(v_f32[:m_prompt].reshape(batch, lpad, sb_heads, SB_DH)[:, :lp])
        sp.append(jnp.swapaxes(s_p, -1, -2))
        ks_.append(kn[m_prompt:].reshape(dec_batch, dec_seq, sb_heads, SB_DH))
        vs_.append(v_f32[m_prompt:].reshape(dec_batch, dec_seq, sb_heads, SB_DH))
        ss_.append(jnp.swapaxes(s_s, -1, -2))

        i = l // 2
        if l % 2 == 0:
            hn = rmsnorm(h, norm_ffn[l], tr)
            pad_c = ((0, 0), (0, d_ffp - d_ff))
            wg = jnp.pad(w_ff_gate[i], pad_c).astype(BF16)
            wu = jnp.pad(w_ff_up[i], pad_c).astype(BF16)
            wd = jnp.pad(w_ff_down[i], ((0, d_ffp - d_ff), (0, 0))).astype(BF16)
            mid = swiglu_up(hn, wg, wu, tm, tn)
            h = matmul_acc_res(mid, wd, h, tm, _tile(d_model, 1024), _tile(d_ffp, 3072), name="ffn_down")
        else:
            hn, top_idx, top_w = rmsnorm_router(h, norm_ffn[l], w_router[i], tr)
            h = moe_ffn(h, hn, top_idx[:, :2], top_w[:, :2], w_moe_gate[i].astype(BF16),
                        w_moe_up[i].astype(BF16), w_moe_down[i].astype(BF16), tn)

    hp = h[:m_prompt].reshape(batch, lpad, d_model)
    y_prompt = hp[:, n_meta:lp]
    y_sample = h[m_prompt:].reshape(dec_batch, dec_seq, d_model)
    return (y_prompt, y_sample, jnp.stack(kp), jnp.stack(vp), jnp.stack(sp),
            jnp.stack(ks_), jnp.stack(vs_), jnp.stack(ss_))
```

```python
import functools

import jax
import jax.numpy as jnp
from jax import lax
from jax.experimental import pallas as pl
from jax.experimental.pallas import tpu as pltpu

F32 = jnp.float32
BF16 = jnp.bfloat16

GLA_DK = 128
GLA_DV = 256
GLA_TAU = 16.0
SB_DH = 128
SB_SCALE = SB_DH ** -0.5
N_EXPERTS = 8
EPS = 1e-6

LANE = 128
V7X_VMEM_BYTES = 64 * 1024 * 1024
VMEM_LIMIT = V7X_VMEM_BYTES - 8 * 1024 * 1024

ROW_ALIGN = 128
GLA_SUB = 32
GLA_HEAD_GROUP = 4
SB_TILE = 256
SB_ZERO_EXP = -110.0
MOE_TILE = 512
MOE_UP_TN = 256
MOE_DOWN_TN = 512
D_FF_ALIGN = 1024

NT_DIMS = (((1,), (1,)), ((), ()))
TN_DIMS = (((0,), (0,)), ((), ()))


def _cparams(sem):
    return pltpu.CompilerParams(dimension_semantics=sem, vmem_limit_bytes=VMEM_LIMIT)


def _round_up(x, m):
    return (x + m - 1) // m * m


def _tile(n, cap, align=LANE):
    best = None
    t = align
    while t <= min(n, cap):
        if n % t == 0:
            best = t
        t += align
    assert best is not None, (n, cap, align)
    return best


def _dot(a, b):
    return jnp.dot(a, b, preferred_element_type=F32)


def _softplus(z):
    return jnp.maximum(z, 0.0) + jnp.log1p(jnp.exp(-jnp.abs(z)))


def _rmsnorm_kernel(x_ref, g_ref, o_ref):
    x = x_ref[...]
    ms = jnp.mean(x * x, axis=-1, keepdims=True)
    o_ref[...] = (x * lax.rsqrt(ms + EPS) * g_ref[...]).astype(o_ref.dtype)


def rmsnorm(x, g, tr):
    m, d = x.shape
    return pl.pallas_call(
        _rmsnorm_kernel,
        grid=(m // tr,),
        in_specs=[pl.BlockSpec((tr, d), lambda i: (i, 0)),
                  pl.BlockSpec((1, d), lambda i: (0, 0))],
        out_specs=pl.BlockSpec((tr, d), lambda i: (i, 0)),
        out_shape=jax.ShapeDtypeStruct((m, d), BF16),
        compiler_params=_cparams(("parallel",)),
        name="rmsnorm",
    )(x, g.reshape(1, d))


def _rmsnorm_router_kernel(x_ref, g_ref, wr_ref, o_ref, idx_ref, wgt_ref):
    x = x_ref[...]
    ms = jnp.mean(x * x, axis=-1, keepdims=True)
    xn = x * lax.rsqrt(ms + EPS) * g_ref[...]
    o_ref[...] = xn.astype(o_ref.dtype)
    logits = jnp.dot(xn, wr_ref[...], preferred_element_type=F32,
                     precision=lax.Precision.HIGHEST)
    lane = lax.broadcasted_iota(jnp.int32, logits.shape, 1).astype(F32)
    neg = jnp.float32(-jnp.inf)
    l0 = jnp.where(lane < N_EXPERTS, logits, neg)
    m0 = jnp.max(l0, axis=-1, keepdims=True)
    i0 = jnp.min(jnp.where(l0 == m0, lane, float(LANE)), axis=-1, keepdims=True)
    l1 = jnp.where(lane == i0, neg, l0)
    m1 = jnp.max(l1, axis=-1, keepdims=True)
    i1 = jnp.min(jnp.where(l1 == m1, lane, float(LANE)), axis=-1, keepdims=True)
    e1 = jnp.exp(m1 - m0)
    den = 1.0 + e1
    idx_ref[...] = jnp.where(lane == 0.0, i0, i1).astype(jnp.int32)
    wgt_ref[...] = jnp.where(lane == 0.0, 1.0 / den, e1 / den)


def rmsnorm_router(x, g, w_router, tr):
    m, d = x.shape
    wr = jnp.pad(w_router.astype(F32), ((0, 0), (0, LANE - w_router.shape[1])))
    row = lambda i: (i, 0)
    return pl.pallas_call(
        _rmsnorm_router_kernel,
        grid=(m // tr,),
        in_specs=[pl.BlockSpec((tr, d), row),
                  pl.BlockSpec((1, d), lambda i: (0, 0)),
                  pl.BlockSpec((d, LANE), lambda i: (0, 0))],
        out_specs=[pl.BlockSpec((tr, d), row),
                   pl.BlockSpec((tr, LANE), row),
                   pl.BlockSpec((tr, LANE), row)],
        out_shape=[jax.ShapeDtypeStruct((m, d), BF16),
                   jax.ShapeDtypeStruct((m, LANE), jnp.int32),
                   jax.ShapeDtypeStruct((m, LANE), F32)],
        compiler_params=_cparams(("parallel",)),
        name="rmsnorm_router",
    )(x, g.reshape(1, d), wr)


def _wtile(w_ref):
    return w_ref[...].astype(BF16)


def _wspec(w, layer, tn):
    if w.ndim == 2:
        return pl.BlockSpec((w.shape[0], tn), lambda i, j: (0, j))
    return pl.BlockSpec((None, w.shape[1], tn), lambda i, j: (layer, 0, j))


def _mm_kernel(x_ref, w_ref, o_ref):
    o_ref[...] = _dot(x_ref[...], _wtile(w_ref)).astype(o_ref.dtype)


def _mm_res_kernel(x_ref, w_ref, r_ref, o_ref):
    o_ref[...] = (r_ref[...] + _dot(x_ref[...], _wtile(w_ref))).astype(o_ref.dtype)


def matmul(x, w, tm, tn, out_dtype, res=None, layer=None, n=None, name="matmul"):
    m, k = x.shape
    n = w.shape[-1] if n is None else n
    in_specs = [pl.BlockSpec((tm, k), lambda i, j: (i, 0)), _wspec(w, layer, tn)]
    args = [x, w]
    kern = _mm_kernel
    if res is not None:
        in_specs.append(pl.BlockSpec((tm, tn), lambda i, j: (i, j)))
        args.append(res)
        kern = _mm_res_kernel
    return pl.pallas_call(
        kern,
        grid=(m // tm, n // tn),
        in_specs=in_specs,
        out_specs=pl.BlockSpec((tm, tn), lambda i, j: (i, j)),
        out_shape=jax.ShapeDtypeStruct((m, n), out_dtype),
        compiler_params=_cparams(("parallel", "arbitrary")),
        name=name,
    )(*args)


def _mm_acc_res_kernel(x_ref, w_ref, r_ref, o_ref, acc_ref):
    kk = pl.program_id(2)

    @pl.when(kk == 0)
    def _():
        acc_ref[...] = jnp.zeros_like(acc_ref)

    acc_ref[...] += _dot(x_ref[...], w_ref[...])

    @pl.when(kk == pl.num_programs(2) - 1)
    def _():
        o_ref[...] = r_ref[...] + acc_ref[...]


def matmul_acc_res(x, w, res, tm, tn, tk, name="matmul_acc"):
    m, k = x.shape
    n = w.shape[1]
    return pl.pallas_call(
        _mm_acc_res_kernel,
        grid=(m // tm, n // tn, k // tk),
        in_specs=[pl.BlockSpec((tm, tk), lambda i, j, kk: (i, kk)),
                  pl.BlockSpec((tk, tn), lambda i, j, kk: (kk, j)),
                  pl.BlockSpec((tm, tn), lambda i, j, kk: (i, j))],
        out_specs=pl.BlockSpec((tm, tn), lambda i, j, kk: (i, j)),
        out_shape=jax.ShapeDtypeStruct((m, n), F32),
        scratch_shapes=[pltpu.VMEM((tm, tn), F32)],
        compiler_params=_cparams(("parallel", "arbitrary", "arbitrary")),
        name=name,
    )(x, w, res)


def _swiglu_kernel(x_ref, wg_ref, wu_ref, o_ref):
    x = x_ref[...]
    a = _dot(x, wg_ref[...])
    u = _dot(x, wu_ref[...])
    o_ref[...] = (a * jax.nn.sigmoid(a) * u).astype(o_ref.dtype)


def swiglu_up(x, wg, wu, tm, tn):
    m, k = x.shape
    n = wg.shape[1]
    wspec = pl.BlockSpec((k, tn), lambda i, j: (0, j))
    return pl.pallas_call(
        _swiglu_kernel,
        grid=(m // tm, n // tn),
        in_specs=[pl.BlockSpec((tm, k), lambda i, j: (i, 0)), wspec, wspec],
        out_specs=pl.BlockSpec((tm, tn), lambda i, j: (i, j)),
        out_shape=jax.ShapeDtypeStruct((m, n), BF16),
        compiler_params=_cparams(("parallel", "arbitrary")),
        name="swiglu_up",
    )(x, wg, wu)


def _merge_kernel(oa_ref, wa_ref, ob_ref, wb_ref, ga_ref, gb_ref, o_ref):
    a = _dot(oa_ref[...], _wtile(wa_ref))
    b = _dot(ob_ref[...], _wtile(wb_ref))
    o_ref[...] = (jax.nn.sigmoid(ga_ref[...]) * a + jax.nn.sigmoid(gb_ref[...]) * b).astype(o_ref.dtype)


def branch_merge(oa, wa, ob, wb, layer, proj, gate_col, tm, tn):
    m, ka = oa.shape
    kb = ob.shape[1]
    n = wa.shape[-1]
    ga0 = gate_col // tn
    gb0 = (gate_col + n) // tn
    return pl.pallas_call(
        _merge_kernel,
        grid=(m // tm, n // tn),
        in_specs=[pl.BlockSpec((tm, ka), lambda i, j: (i, 0)),
                  _wspec(wa, layer, tn),
                  pl.BlockSpec((tm, kb), lambda i, j: (i, 0)),
                  _wspec(wb, layer, tn),
                  pl.BlockSpec((tm, tn), lambda i, j: (i, ga0 + j)),
                  pl.BlockSpec((tm, tn), lambda i, j: (i, gb0 + j))],
        out_specs=pl.BlockSpec((tm, tn), lambda i, j: (i, j)),
        out_shape=jax.ShapeDtypeStruct((m, n), BF16),
        compiler_params=_cparams(("parallel", "arbitrary")),
        name="branch_merge",
    )(oa, wa, ob, wb, proj, proj)


def _alpha_kernel(x_ref, w1_ref, w2_ref, b_ref, o_ref):
    a = _dot(x_ref[...], w1_ref[...])
    z = _dot(a.astype(BF16), w2_ref[...]) + b_ref[...]
    o_ref[...] = (jnp.minimum(z, 0.0) - jnp.log1p(jnp.exp(-jnp.abs(z)))) * (1.0 / GLA_TAU)


def gla_log_decay(xn, w1, w2, b, tr):
    m, d = xn.shape
    kw = w2.shape[1]
    return pl.pallas_call(
        _alpha_kernel,
        grid=(m // tr,),
        in_specs=[pl.BlockSpec((tr, d), lambda i: (i, 0)),
                  pl.BlockSpec((d, LANE), lambda i: (0, 0)),
                  pl.BlockSpec((LANE, kw), lambda i: (0, 0)),
                  pl.BlockSpec((1, kw), lambda i: (0, 0))],
        out_specs=pl.BlockSpec((tr, kw), lambda i: (i, 0)),
        out_shape=jax.ShapeDtypeStruct((m, kw), F32),
        compiler_params=_cparams(("parallel",)),
        name="gla_log_decay",
    )(xn, w1, w2, b.reshape(1, kw))


def _gla_kernel(q_ref, k_ref, v_ref, r_ref, lg_ref, s0_ref, g_ref, o_ref, sout_ref, s_scr,
                *, chunk, heads, n_valid):
    c = pl.program_id(2)

    @pl.when(c == 0)
    def _():
        s_scr[...] = s0_ref[...]

    valid = None
    if n_valid is not None:
        valid = c * chunk + lax.broadcasted_iota(jnp.int32, (chunk, 1), 0) < n_valid
    ri = lax.broadcasted_iota(jnp.int32, (chunk, chunk), 0)
    ci = lax.broadcasted_iota(jnp.int32, (chunk, chunk), 1)
    tril = (ci <= ri).astype(BF16)
    g = g_ref[...]

    for hh in range(heads):
        dk = slice(hh * GLA_DK, (hh + 1) * GLA_DK)
        dv = slice(hh * GLA_DV, (hh + 1) * GLA_DV)
        o, s_new = _gla_head(q_ref[:, dk], k_ref[:, dk], v_ref[:, dv], lg_ref[:, dk], s_scr[hh],
                             tril, valid, chunk)
        s_scr[hh] = s_new
        sout_ref[hh] = s_new
        ms = jnp.mean(o * o, axis=-1, keepdims=True)
        on = o * lax.rsqrt(ms + EPS) * g
        r = r_ref[:, dv]
        o_ref[:, dv] = (on * (r * jax.nn.sigmoid(r))).astype(o_ref.dtype)


def _gla_head(q, k, v, lg, st, tril, valid, chunk):
    q = q * (GLA_DK ** -0.5)
    if valid is not None:
        lg = jnp.where(valid, lg, 0.0)
        k = jnp.where(valid, k, 0.0)

    hi = lg.astype(BF16)
    r1 = lg - hi.astype(F32)
    mid = r1.astype(BF16)
    lo = (r1 - mid.astype(F32)).astype(BF16)
    b = _dot(tril, hi) + _dot(tril, mid) + _dot(tril, lo)

    vb = v.astype(BF16)
    o_inter = lax.dot_general((q * jnp.exp(b)).astype(BF16), st.astype(BF16), NT_DIMS,
                              preferred_element_type=F32)

    outs = []
    for i in range(chunk // GLA_SUB):
        lo_r = i * GLA_SUB
        hi_r = lo_r + GLA_SUB
        mid_r = lo_r + GLA_SUB // 2 - 1
        ref_b = b[mid_r:mid_r + 1, :]
        qs = (q[lo_r:hi_r] * jnp.exp(b[lo_r:hi_r] - ref_b)).astype(BF16)
        ks = (k[:hi_r] * jnp.exp(ref_b - b[:hi_r])).astype(BF16)
        a = lax.dot_general(qs, ks, NT_DIMS, preferred_element_type=F32)
        t_idx = lo_r + lax.broadcasted_iota(jnp.int32, (GLA_SUB, hi_r), 0)
        s_idx = lax.broadcasted_iota(jnp.int32, (GLA_SUB, hi_r), 1)
        a = jnp.where(s_idx <= t_idx, a, 0.0)
        outs.append(o_inter[lo_r:hi_r] + _dot(a.astype(BF16), vb[:hi_r]))
    o = jnp.concatenate(outs, axis=0)

    b_last = b[chunk - 1:chunk, :]
    kd = (k * jnp.exp(b_last - b)).astype(BF16)
    s_new = st * jnp.exp(b_last) + lax.dot_general(vb, kd, TN_DIMS, preferred_element_type=F32)
    return o, s_new


def gla(proj, lg, s0, g, *, row0, n_seq, n_chunks, chunk, n_valid, n_heads, k_col, v_col, r_col):
    hg = GLA_HEAD_GROUP
    wk, wv = hg * GLA_DK, hg * GLA_DV
    assert n_heads % hg == 0 and k_col % wk == 0 and v_col % wv == 0 and r_col % wv == 0
    base = row0 // chunk
    kc, vc, rc = k_col // wk, v_col // wv, r_col // wv
    rowblk = lambda s, c: base + s * n_chunks + c
    kern = functools.partial(_gla_kernel, chunk=chunk, heads=hg,
                             n_valid=None if n_valid == n_chunks * chunk else n_valid)
    state_spec = pl.BlockSpec((None, hg, GLA_DV, GLA_DK), lambda s, h, c: (s, h, 0, 0))
    return pl.pallas_call(
        kern,
        grid=(n_seq, n_heads // hg, n_chunks),
        in_specs=[pl.BlockSpec((chunk, wk), lambda s, h, c: (rowblk(s, c), h)),
                  pl.BlockSpec((chunk, wk), lambda s, h, c: (rowblk(s, c), kc + h)),
                  pl.BlockSpec((chunk, wv), lambda s, h, c: (rowblk(s, c), vc + h)),
                  pl.BlockSpec((chunk, wv), lambda s, h, c: (rowblk(s, c), rc + h)),
                  pl.BlockSpec((chunk, wk), lambda s, h, c: (rowblk(s, c), h)),
                  state_spec,
                  pl.BlockSpec((1, GLA_DV), lambda s, h, c: (0, 0))],
        out_specs=[pl.BlockSpec((chunk, wv), lambda s, h, c: (s * n_chunks + c, h)),
                   state_spec],
        out_shape=[jax.ShapeDtypeStruct((n_seq * n_chunks * chunk, n_heads * GLA_DV), BF16),
                   jax.ShapeDtypeStruct((n_seq, n_heads, GLA_DV, GLA_DK), F32)],
        scratch_shapes=[pltpu.VMEM((hg, GLA_DV, GLA_DK), F32)],
        compiler_params=_cparams(("parallel", "parallel", "arbitrary")),
        name="gla_chunk",
    )(proj, proj, proj, proj, lg, s0, g.reshape(1, GLA_DV))


def _qknorm_kernel(q_ref, k_ref, v_ref, qg_ref, kg_ref, ko_ref, vo_ref, qh_ref, kh_ref, vh_ref,
                   *, n_heads, tr):
    qg = qg_ref[...]
    kg = kg_ref[...]
    for h in range(n_heads):
        sl = slice(h * SB_DH, (h + 1) * SB_DH)
        q = q_ref[:, sl]
        k = k_ref[:, sl]
        v = v_ref[:, sl]
        qn = q * lax.rsqrt(jnp.mean(q * q, axis=-1, keepdims=True) + EPS) * qg
        kn = k * lax.rsqrt(jnp.mean(k * k, axis=-1, keepdims=True) + EPS) * kg
        ko_ref[pl.ds(h, tr, stride=n_heads), :] = kn
        vo_ref[pl.ds(h, tr, stride=n_heads), :] = v
        qh_ref[:, sl] = (qn * SB_SCALE).astype(BF16)
        kh_ref[:, sl] = kn.astype(BF16)
        vh_ref[:, sl] = v.astype(BF16)


def qk_norm(proj, qg, kg, q_col, sb_w, tr):
    m = proj.shape[0]
    n_heads = sb_w // SB_DH
    c0 = q_col // sb_w
    row = lambda i: (i, 0)
    out = jax.ShapeDtypeStruct((m, sb_w), BF16)
    cache = jax.ShapeDtypeStruct((m * n_heads, SB_DH), F32)
    return pl.pallas_call(
        functools.partial(_qknorm_kernel, n_heads=n_heads, tr=tr),
        grid=(m // tr,),
        in_specs=[pl.BlockSpec((tr, sb_w), lambda i: (i, c0)),
                  pl.BlockSpec((tr, sb_w), lambda i: (i, c0 + 1)),
                  pl.BlockSpec((tr, sb_w), lambda i: (i, c0 + 2)),
                  pl.BlockSpec((1, SB_DH), lambda i: (0, 0)),
                  pl.BlockSpec((1, SB_DH), lambda i: (0, 0))],
        out_specs=[pl.BlockSpec((tr * n_heads, SB_DH), row)] * 2 + [pl.BlockSpec((tr, sb_w), row)] * 3,
        out_shape=[cache, cache, out, out, out],
        compiler_params=_cparams(("parallel",)),
        name="qk_norm",
    )(proj, proj, proj, qg.reshape(1, SB_DH), kg.reshape(1, SB_DH))


def _sb_tile(q, kblk, vblk, trir, carry, vis):
    acc, rest = carry
    z = lax.dot_general(q, kblk, NT_DIMS, preferred_element_type=F32)
    sp = _softplus(z)
    if vis is not None:
        sp = jnp.where(vis, sp, 0.0)
    hi = sp.astype(BF16)
    lo = (sp - hi.astype(F32)).astype(BF16)
    cs = _dot(hi, trir) + _dot(lo, trir)
    w = jnp.exp(z - cs - rest)
    if vis is not None:
        w = jnp.where(vis, w, 0.0)
    acc = acc + _dot(w.astype(BF16), vblk)
    return acc, rest + cs[:, 0:1]


def _rev_tri(n):
    s = lax.broadcasted_iota(jnp.int32, (n, n), 0)
    r = lax.broadcasted_iota(jnp.int32, (n, n), 1)
    return (s >= r).astype(BF16)


def _strict_causal(tq, tk):
    t = lax.broadcasted_iota(jnp.int32, (tq, tk), 0)
    s = lax.broadcasted_iota(jnp.int32, (tq, tk), 1)
    return s < t


def _key_norm_max(k_ref, rows, step):
    def body(c, m):
        off = pl.multiple_of(c * step, step)
        kf = k_ref[pl.ds(off, step), :].astype(F32)
        return jnp.maximum(m, jnp.sum(kf * kf, axis=-1, keepdims=True))

    m = lax.fori_loop(0, rows // step, body, jnp.zeros((step, 1), F32))
    return jnp.sqrt(jnp.max(m))


def _sb_sweep(q, acc, rest, n_tiles, k_norm_max, load_tile, trir):
    qf = q.astype(F32)
    q_norm_max = jnp.sqrt(jnp.max(jnp.sum(qf * qf, axis=-1, keepdims=True)))
    z_bound = q_norm_max * k_norm_max * 1.01 + 1.0

    def live(rest):
        return jnp.logical_not(z_bound - jnp.min(rest) < SB_ZERO_EXP).astype(jnp.int32)

    def cond(c):
        return jnp.logical_and(c[0] >= 0, c[3] != 0)

    def body(c):
        j, acc, rest, _ = c
        kblk, vblk = load_tile(j)
        acc, rest = _sb_tile(q, kblk, vblk, trir, (acc, rest), None)
        return j - 1, acc, rest, live(rest)

    return lax.while_loop(cond, body, (n_tiles - 1, acc, rest, live(rest)))[1]


def _sb_prompt_kernel(q_ref, k_ref, v_ref, tri_ref, o_ref, kmax_ref, *, tile, seq_rows):
    i = pl.program_id(2)

    @pl.when(i == 0)
    def _():
        kmax_ref[0] = _key_norm_max(k_ref, seq_rows, tile)

    q = q_ref[...]
    trir = tri_ref[...]

    def load(j):
        off = pl.multiple_of(j * tile, tile)
        return k_ref[pl.ds(off, tile), :], v_ref[pl.ds(off, tile), :]

    carry = (jnp.zeros((tile, SB_DH), F32), jnp.zeros((tile, 1), F32))
    kd, vd = load(i)
    acc, rest = _sb_tile(q, kd, vd, trir, carry, _strict_causal(tile, tile))
    acc = _sb_sweep(q, acc, rest, i, kmax_ref[0], load, trir)
    o_ref[...] = acc.astype(o_ref.dtype)


def sb_prompt(qh, kh, vh, tri, *, n_seq, seq_rows, n_heads):
    tile = SB_TILE
    nqt = seq_rows // tile
    kv_spec = pl.BlockSpec((seq_rows, SB_DH), lambda s, h, i: (s, h))
    q_spec = pl.BlockSpec((tile, SB_DH), lambda s, h, i: (s * nqt + i, h))
    return pl.pallas_call(
        functools.partial(_sb_prompt_kernel, tile=tile, seq_rows=seq_rows),
        grid=(n_seq, n_heads, nqt),
        in_specs=[q_spec, kv_spec, kv_spec, pl.BlockSpec((tile, tile), lambda s, h, i: (0, 0))],
        out_specs=q_spec,
        out_shape=jax.ShapeDtypeStruct((n_seq * seq_rows, n_heads * SB_DH), BF16),
        scratch_shapes=[pltpu.SMEM((1,), F32)],
        compiler_params=_cparams(("parallel", "parallel", "arbitrary")),
        name="sb_prompt",
    )(qh, kh, vh, tri)


def _sb_sample_kernel(q_ref, kn_ref, vn_ref, kc_ref, vc_ref, tri_ref, o_ref, *, t_new, past, tile):
    q = q_ref[...]
    carry = (jnp.zeros((t_new, SB_DH), F32), jnp.zeros((t_new, 1), F32))
    acc, rest = _sb_tile(q, kn_ref[...], vn_ref[...], _rev_tri(t_new), carry, _strict_causal(t_new, t_new))

    def load(j):
        off = pl.multiple_of(j * tile, tile)
        return kc_ref[pl.ds(off, tile), :].astype(BF16), vc_ref[pl.ds(off, tile), :].astype(BF16)

    acc = _sb_sweep(q, acc, rest, past // tile, _key_norm_max(kc_ref, past, tile), load, tri_ref[...])
    o_ref[...] = acc.astype(o_ref.dtype)


def sb_sample(qh, kh, vh, cache_k, cache_v, tri, layer, *, row0, n_seq, t_new, n_heads):
    past = cache_k.shape[2]
    tile = SB_TILE
    assert past % tile == 0 and row0 % t_new == 0
    base = row0 // t_new
    new_spec = pl.BlockSpec((t_new, SB_DH), lambda s, h: (base + s, h))
    cache_spec = pl.BlockSpec((None, None, past, SB_DH), lambda s, h: (layer, s, 0, h))
    return pl.pallas_call(
        functools.partial(_sb_sample_kernel, t_new=t_new, past=past, tile=tile),
        grid=(n_seq, n_heads),
        in_specs=[new_spec, new_spec, new_spec, cache_spec, cache_spec,
                  pl.BlockSpec((tile, tile), lambda s, h: (0, 0))],
        out_specs=pl.BlockSpec((t_new, SB_DH), lambda s, h: (s, h)),
        out_shape=jax.ShapeDtypeStruct((n_seq * t_new, n_heads * SB_DH), BF16),
        compiler_params=_cparams(("parallel", "parallel")),
        name="sb_sample",
    )(qh, kh, vh, cache_k, cache_v, tri)


def _moe_up_kernel(te_ref, tv_ref, tf_ref, x_ref, wg_ref, wu_ref, o_ref, wg_s, wu_s):
    i = pl.program_id(1)

    @pl.when(tf_ref[i] != 0)
    def _():
        wg_s[...] = wg_ref[...].astype(BF16)
        wu_s[...] = wu_ref[...].astype(BF16)

    @pl.when(tv_ref[i] != 0)
    def _():
        x = x_ref[...]
        a = _dot(x, wg_s[...])
        u = _dot(x, wu_s[...])
        o_ref[...] = (a * jax.nn.sigmoid(a) * u).astype(o_ref.dtype)

    @pl.when(tv_ref[i] == 0)
    def _():
        o_ref[...] = jnp.zeros_like(o_ref)


def _moe_down_kernel(te_ref, tv_ref, tf_ref, x_ref, w_ref, o_ref, w_s):
    i = pl.program_id(1)

    @pl.when(tf_ref[i] != 0)
    def _():
        w_s[...] = w_ref[...].astype(BF16)

    @pl.when(tv_ref[i] != 0)
    def _():
        o_ref[...] = _dot(x_ref[...], w_s[...])

    @pl.when(tv_ref[i] == 0)
    def _():
        o_ref[...] = jnp.zeros_like(o_ref)


def _moe_call(kern, x, ws, layer, tiles, tm, tn, out_dtype, name):
    n_rows, k = x.shape
    n = ws[0].shape[-1]
    wspec = pl.BlockSpec((None, None, k, tn), lambda j, i, te, tv, tf: (layer, te[i], 0, j))
    return pl.pallas_call(
        kern,
        grid_spec=pltpu.PrefetchScalarGridSpec(
            num_scalar_prefetch=3,
            grid=(n // tn, n_rows // tm),
            in_specs=[pl.BlockSpec((tm, k), lambda j, i, te, tv, tf: (i, 0))] + [wspec] * len(ws),
            out_specs=pl.BlockSpec((tm, tn), lambda j, i, te, tv, tf: (i, j)),
            scratch_shapes=[pltpu.VMEM((k, tn), BF16)] * len(ws)),
        out_shape=jax.ShapeDtypeStruct((n_rows, n), out_dtype),
        compiler_params=_cparams(("arbitrary", "arbitrary")),
        name=name,
    )(*tiles, x, *ws)


def moe_ffn(h, hn, top_idx, top_w, wg, wu, wd, layer):
    m = hn.shape[0]
    n_assign = 2 * m
    tm = MOE_TILE
    n_rows = _round_up(n_assign + N_EXPERTS * (tm - 1), tm)
    n_tiles = n_rows // tm
    e_flat = top_idx.reshape(-1)
    order = jnp.argsort(e_flat, stable=True)
    e_sorted = e_flat[order]
    counts = jnp.zeros(N_EXPERTS, jnp.int32).at[e_flat].add(1)
    padded = (counts + tm - 1) // tm * tm
    start = jnp.cumsum(counts) - counts
    pend = jnp.cumsum(padded)
    pstart = pend - padded
    dest = (pstart[e_sorted] + (jnp.arange(n_assign, dtype=jnp.int32) - start[e_sorted])).astype(jnp.int32)
    row_tok = jnp.zeros(n_rows, jnp.int32).at[dest].set((order // 2).astype(jnp.int32))
    pos = jnp.zeros(n_assign, jnp.int32).at[order].set(dest).reshape(m, 2)
    tile_start = jnp.arange(n_tiles, dtype=jnp.int32) * tm
    tile_valid = (tile_start < pend[-1]).astype(jnp.int32)
    last_valid = jnp.maximum(pend[-1] // tm - 1, 0)
    tile_expert = jnp.clip(jnp.searchsorted(pend, tile_start, side="right"), 0, N_EXPERTS - 1).astype(jnp.int32)
    tile_expert = jnp.where(tile_valid != 0, tile_expert, tile_expert[last_valid])
    tile_first = jnp.concatenate([jnp.ones((1,), jnp.int32),
                                  (tile_expert[1:] != tile_expert[:-1]).astype(jnp.int32)])
    tiles = (tile_expert, tile_valid, tile_first)

    xg = hn[row_tok]
    hg = _moe_call(_moe_up_kernel, xg, (wg, wu), layer, tiles, tm, MOE_UP_TN, BF16, "moe_up")
    yb = _moe_call(_moe_down_kernel, hg, (wd,), layer, tiles, tm, MOE_DOWN_TN, F32, "moe_down")
    return h + (yb[pos[:, 0]] * top_w[:, 0:1] + yb[pos[:, 1]] * top_w[:, 1:2])


def kernel(x_prompt, x_sample, cache_k_sb, cache_v_sb, state_gla, meta_tokens, norm_mix, norm_ffn, w_in, w_alpha2, b_alpha, gla_norm, q_norm, k_norm, w_branch_a, w_branch_b, w_out, w_ff_gate, w_ff_up, w_ff_down, w_router, w_moe_gate, w_moe_up, w_moe_down):
    batch, seq, d_model = x_prompt.shape
    dec_batch, dec_seq, _ = x_sample.shape
    depth = w_in.shape[0]
    n_meta = meta_tokens.shape[0]
    past = cache_k_sb.shape[2]
    gla_rank, gla_kw = w_alpha2.shape[1], w_alpha2.shape[2]
    gla_w = w_branch_a.shape[1]
    sb_w = w_branch_b.shape[1]
    gla_heads = gla_w // GLA_DV
    sb_heads = sb_w // SB_DH
    d_ff = w_ff_gate.shape[2]
    assert gla_kw == gla_heads * GLA_DK and dec_seq % GLA_SUB == 0

    alr_col = 2 * gla_kw + 2 * gla_w
    k_col, v_col, r_col = gla_kw, 2 * gla_kw, 2 * gla_kw + gla_w
    gate_col = 3 * sb_w

    lp = n_meta + seq
    lpad = _round_up(lp, ROW_ALIGN)
    m_prompt = batch * lpad
    m_sample = dec_batch * dec_seq
    m_tot = m_prompt + m_sample
    tm = _tile(m_tot, 1024)
    tr = _tile(m_tot, 256)
    tn = 512

    meta = jnp.broadcast_to(meta_tokens[None].astype(F32), (batch, n_meta, d_model))
    hp = jnp.concatenate([meta, x_prompt], axis=1)
    hp = jnp.pad(hp, ((0, 0), (0, lpad - lp), (0, 0))).reshape(m_prompt, d_model)
    h = jnp.concatenate([hp, x_sample.reshape(m_sample, d_model)], axis=0)

    cache_k = cache_k_sb.reshape(depth, dec_batch, past, sb_w)
    cache_v = cache_v_sb.reshape(depth, dec_batch, past, sb_w)
    s0_prompt = jnp.zeros((batch, gla_heads, GLA_DV, GLA_DK), F32)

    d_ffp = _round_up(d_ff, D_FF_ALIGN)

    lpa = _round_up(lpad, SB_TILE)
    tri = jnp.tril(jnp.ones((SB_TILE, SB_TILE), BF16))

    def attn_rows(a):
        p = a[:m_prompt].reshape(batch, lpad, sb_w)
        return jnp.pad(p, ((0, 0), (0, lpa - lpad), (0, 0))).reshape(batch * lpa, sb_w)

    kp, vp, sp, ks_, vs_, ss_ = [], [], [], [], [], []
    for l in range(depth):
        w_b = w_in[l, :, alr_col + gla_rank:].astype(BF16)
        w_alr = jnp.pad(w_in[l, :, alr_col:alr_col + gla_rank], ((0, 0), (0, LANE - gla_rank))).astype(BF16)
        w_a2 = jnp.pad(w_alpha2[l], ((0, LANE - gla_rank), (0, 0))).astype(BF16)

        xn = rmsnorm(h, norm_mix[l], tr)
        proj_a = matmul(xn, w_in, tm, tn, F32, layer=l, n=alr_col, name="in_proj_gla")
        proj_b = matmul(xn, w_b, tm, tn, F32, name="in_proj_sb")
        lg = gla_log_decay(xn, w_alr, w_a2, b_alpha[l], tm)

        gla_cols = dict(n_heads=gla_heads, k_col=k_col, v_col=v_col, r_col=r_col)
        oa_p, s_p = gla(proj_a, lg, s0_prompt, gla_norm[l], row0=0, n_seq=batch, n_chunks=lpad // ROW_ALIGN,
                        chunk=ROW_ALIGN, n_valid=lp, **gla_cols)
        oa_s, s_s = gla(proj_a, lg, jnp.swapaxes(state_gla[l], -1, -2), gla_norm[l], row0=m_prompt,
                        n_seq=dec_batch, n_chunks=1, chunk=dec_seq, n_valid=dec_seq, **gla_cols)
        oa = jnp.concatenate([oa_p, oa_s], axis=0)

        k_out, v_out, qh, kh, vh = qk_norm(proj_b, q_norm[l], k_norm[l], 0, sb_w, tr)
        ob_p = sb_prompt(attn_rows(qh), attn_rows(kh), attn_rows(vh), tri, n_seq=batch, seq_rows=lpa,
                         n_heads=sb_heads)
        ob_p = ob_p.reshape(batch, lpa, sb_w)[:, :lpad].reshape(m_prompt, sb_w)
        ob_s = sb_sample(qh, kh, vh, cache_k, cache_v, tri, l, row0=m_prompt, n_seq=dec_batch, t_new=dec_seq,
                         n_heads=sb_heads)
        ob = jnp.concatenate([ob_p, ob_s], axis=0)

        merged = branch_merge(oa, w_branch_a, ob, w_branch_b, l, proj_b, gate_col, tm, tn)
        h = matmul(merged, w_out, tm, tn, F32, res=h, layer=l, name="out_proj")

        prompt_rows = lambda a: a[:m_prompt * sb_heads].reshape(batch, lpad, sb_heads, SB_DH)[:, :lp]
        sample_rows = lambda a: a[m_prompt * sb_heads:].reshape(dec_batch, dec_seq, sb_heads, SB_DH)
        kp.append(prompt_rows(k_out))
        vp.append(prompt_rows(v_out))
        sp.append(jnp.swapaxes(s_p, -1, -2))
        ks_.append(sample_rows(k_out))
        vs_.append(sample_rows(v_out))
        ss_.append(jnp.swapaxes(s_s, -1, -2))

        i = l // 2
        if l % 2 == 0:
            hn = rmsnorm(h, norm_ffn[l], tr)
            pad_c = ((0, 0), (0, d_ffp - d_ff))
            wg = jnp.pad(w_ff_gate[i], pad_c).astype(BF16)
            wu = jnp.pad(w_ff_up[i], pad_c).astype(BF16)
            wd = jnp.pad(w_ff_down[i], ((0, d_ffp - d_ff), (0, 0))).astype(BF16)
            mid = swiglu_up(hn, wg, wu, tm, tn)
            h = matmul_acc_res(mid, wd, h, tm, _tile(d_model, 1024), _tile(d_ffp, 3072), name="ffn_down")
        else:
            hn, top_idx, top_w = rmsnorm_router(h, norm_ffn[l], w_router[i], tr)
            h = moe_ffn(h, hn, top_idx[:, :2], top_w[:, :2], w_moe_gate, w_moe_up, w_moe_down, i)

    hp = h[:m_prompt].reshape(batch, lpad, d_model)
    y_prompt = hp[:, n_meta:lp]
    y_sample = h[m_prompt:].reshape(dec_batch, dec_seq, d_model)
    return (y_prompt, y_sample, jnp.stack(kp), jnp.stack(vp), jnp.stack(sp),
            jnp.stack(ks_), jnp.stack(vs_), jnp.stack(ss_))
```

```python
import functools

import jax
import jax.numpy as jnp
from jax import lax
from jax.experimental import pallas as pl
from jax.experimental.pallas import tpu as pltpu

F32 = jnp.float32
BF16 = jnp.bfloat16

GLA_DK = 128
GLA_DV = 256
GLA_TAU = 16.0
SB_DH = 128
SB_SCALE = SB_DH ** -0.5
N_EXPERTS = 8
EPS = 1e-6

LANE = 128
V7X_VMEM_BYTES = 64 * 1024 * 1024
VMEM_LIMIT = V7X_VMEM_BYTES - 8 * 1024 * 1024

ROW_ALIGN = 128
GLA_SUB = 32
GLA_HEAD_GROUP = 12
SB_TILE = 256
SB_ZERO_EXP = -110.0
MOE_TILE = 512
MOE_UP_TN = 512
MOE_DOWN_TN = 512
D_FF_ALIGN = 1024

NT_DIMS = (((1,), (1,)), ((), ()))
TN_DIMS = (((0,), (0,)), ((), ()))


def _cparams(sem):
    return pltpu.CompilerParams(dimension_semantics=sem, vmem_limit_bytes=VMEM_LIMIT)


def _round_up(x, m):
    return (x + m - 1) // m * m


def _tile(n, cap, align=LANE):
    best = None
    t = align
    while t <= min(n, cap):
        if n % t == 0:
            best = t
        t += align
    assert best is not None, (n, cap, align)
    return best


def _dot(a, b):
    return jnp.dot(a, b, preferred_element_type=F32)


def _softplus(z):
    return jnp.maximum(z, 0.0) + jnp.log1p(jnp.exp(-jnp.abs(z)))


def _rmsnorm_kernel(x_ref, g_ref, o_ref):
    x = x_ref[...]
    ms = jnp.mean(x * x, axis=-1, keepdims=True)
    o_ref[...] = (x * lax.rsqrt(ms + EPS) * g_ref[...]).astype(o_ref.dtype)


def rmsnorm(x, g, tr):
    m, d = x.shape
    return pl.pallas_call(
        _rmsnorm_kernel,
        grid=(m // tr,),
        in_specs=[pl.BlockSpec((tr, d), lambda i: (i, 0)),
                  pl.BlockSpec((1, d), lambda i: (0, 0))],
        out_specs=pl.BlockSpec((tr, d), lambda i: (i, 0)),
        out_shape=jax.ShapeDtypeStruct((m, d), BF16),
        compiler_params=_cparams(("parallel",)),
        name="rmsnorm",
    )(x, g.reshape(1, d))


def _rmsnorm_router_kernel(x_ref, g_ref, wr_ref, o_ref, idx_ref, wgt_ref):
    x = x_ref[...]
    ms = jnp.mean(x * x, axis=-1, keepdims=True)
    xn = x * lax.rsqrt(ms + EPS) * g_ref[...]
    o_ref[...] = xn.astype(o_ref.dtype)
    logits = jnp.dot(xn, wr_ref[...], preferred_element_type=F32,
                     precision=lax.Precision.HIGHEST)
    lane = lax.broadcasted_iota(jnp.int32, logits.shape, 1).astype(F32)
    neg = jnp.float32(-jnp.inf)
    l0 = jnp.where(lane < N_EXPERTS, logits, neg)
    m0 = jnp.max(l0, axis=-1, keepdims=True)
    i0 = jnp.min(jnp.where(l0 == m0, lane, float(LANE)), axis=-1, keepdims=True)
    l1 = jnp.where(lane == i0, neg, l0)
    m1 = jnp.max(l1, axis=-1, keepdims=True)
    i1 = jnp.min(jnp.where(l1 == m1, lane, float(LANE)), axis=-1, keepdims=True)
    e1 = jnp.exp(m1 - m0)
    den = 1.0 + e1
    idx_ref[...] = jnp.where(lane == 0.0, i0, i1).astype(jnp.int32)
    wgt_ref[...] = jnp.where(lane == 0.0, 1.0 / den, e1 / den)


def rmsnorm_router(x, g, w_router, tr):
    m, d = x.shape
    wr = jnp.pad(w_router.astype(F32), ((0, 0), (0, LANE - w_router.shape[1])))
    row = lambda i: (i, 0)
    return pl.pallas_call(
        _rmsnorm_router_kernel,
        grid=(m // tr,),
        in_specs=[pl.BlockSpec((tr, d), row),
                  pl.BlockSpec((1, d), lambda i: (0, 0)),
                  pl.BlockSpec((d, LANE), lambda i: (0, 0))],
        out_specs=[pl.BlockSpec((tr, d), row),
                   pl.BlockSpec((tr, LANE), row),
                   pl.BlockSpec((tr, LANE), row)],
        out_shape=[jax.ShapeDtypeStruct((m, d), BF16),
                   jax.ShapeDtypeStruct((m, LANE), jnp.int32),
                   jax.ShapeDtypeStruct((m, LANE), F32)],
        compiler_params=_cparams(("parallel",)),
        name="rmsnorm_router",
    )(x, g.reshape(1, d), wr)


def _wtile(w_ref):
    return w_ref[...].astype(BF16)


def _wspec(w, layer, tn):
    if w.ndim == 2:
        return pl.BlockSpec((w.shape[0], tn), lambda i, j: (0, j))
    return pl.BlockSpec((None, w.shape[1], tn), lambda i, j: (layer, 0, j))


def _mm_kernel(x_ref, w_ref, o_ref):
    o_ref[...] = _dot(x_ref[...], _wtile(w_ref)).astype(o_ref.dtype)


def _mm_nt_kernel(x_ref, wt_ref, o_ref):
    o_ref[...] = lax.dot_general(x_ref[...], _wtile(wt_ref), NT_DIMS,
                                 preferred_element_type=F32).astype(o_ref.dtype)


def _mm_res_kernel(x_ref, w_ref, r_ref, o_ref):
    o_ref[...] = (r_ref[...] + _dot(x_ref[...], _wtile(w_ref))).astype(o_ref.dtype)


def matmul_nt(x, wt, tm, tn, out_dtype, layer=None, n=None, name="matmul_nt"):
    m, k = x.shape
    n = wt.shape[-2] if n is None else n
    if wt.ndim == 2:
        wspec = pl.BlockSpec((tn, k), lambda i, j: (j, 0))
    else:
        wspec = pl.BlockSpec((None, tn, k), lambda i, j: (layer, j, 0))
    return pl.pallas_call(
        _mm_nt_kernel,
        grid=(m // tm, n // tn),
        in_specs=[pl.BlockSpec((tm, k), lambda i, j: (i, 0)), wspec],
        out_specs=pl.BlockSpec((tm, tn), lambda i, j: (i, j)),
        out_shape=jax.ShapeDtypeStruct((m, n), out_dtype),
        compiler_params=_cparams(("parallel", "arbitrary")),
        name=name,
    )(x, wt)


def matmul(x, w, tm, tn, out_dtype, res=None, layer=None, n=None, name="matmul"):
    m, k = x.shape
    n = w.shape[-1] if n is None else n
    in_specs = [pl.BlockSpec((tm, k), lambda i, j: (i, 0)), _wspec(w, layer, tn)]
    args = [x, w]
    kern = _mm_kernel
    if res is not None:
        in_specs.append(pl.BlockSpec((tm, tn), lambda i, j: (i, j)))
        args.append(res)
        kern = _mm_res_kernel
    return pl.pallas_call(
        kern,
        grid=(m // tm, n // tn),
        in_specs=in_specs,
        out_specs=pl.BlockSpec((tm, tn), lambda i, j: (i, j)),
        out_shape=jax.ShapeDtypeStruct((m, n), out_dtype),
        compiler_params=_cparams(("parallel", "arbitrary")),
        name=name,
    )(*args)


def _mm_acc_res_kernel(x_ref, w_ref, r_ref, o_ref, acc_ref):
    kk = pl.program_id(2)

    @pl.when(kk == 0)
    def _():
        acc_ref[...] = jnp.zeros_like(acc_ref)

    acc_ref[...] += _dot(x_ref[...], w_ref[...])

    @pl.when(kk == pl.num_programs(2) - 1)
    def _():
        o_ref[...] = r_ref[...] + acc_ref[...]


def matmul_acc_res(x, w, res, tm, tn, tk, name="matmul_acc"):
    m, k = x.shape
    n = w.shape[1]
    return pl.pallas_call(
        _mm_acc_res_kernel,
        grid=(m // tm, n // tn, k // tk),
        in_specs=[pl.BlockSpec((tm, tk), lambda i, j, kk: (i, kk)),
                  pl.BlockSpec((tk, tn), lambda i, j, kk: (kk, j)),
                  pl.BlockSpec((tm, tn), lambda i, j, kk: (i, j))],
        out_specs=pl.BlockSpec((tm, tn), lambda i, j, kk: (i, j)),
        out_shape=jax.ShapeDtypeStruct((m, n), F32),
        scratch_shapes=[pltpu.VMEM((tm, tn), F32)],
        compiler_params=_cparams(("parallel", "arbitrary", "arbitrary")),
        name=name,
    )(x, w, res)


def _swiglu_kernel(x_ref, wg_ref, wu_ref, o_ref):
    x = x_ref[...]
    a = _dot(x, wg_ref[...])
    u = _dot(x, wu_ref[...])
    o_ref[...] = (a * jax.nn.sigmoid(a) * u).astype(o_ref.dtype)


def swiglu_up(x, wg, wu, tm, tn):
    m, k = x.shape
    n = wg.shape[1]
    wspec = pl.BlockSpec((k, tn), lambda i, j: (0, j))
    return pl.pallas_call(
        _swiglu_kernel,
        grid=(m // tm, n // tn),
        in_specs=[pl.BlockSpec((tm, k), lambda i, j: (i, 0)), wspec, wspec],
        out_specs=pl.BlockSpec((tm, tn), lambda i, j: (i, j)),
        out_shape=jax.ShapeDtypeStruct((m, n), BF16),
        compiler_params=_cparams(("parallel", "arbitrary")),
        name="swiglu_up",
    )(x, wg, wu)


def _merge_kernel(oa_ref, wa_ref, ob_ref, wb_ref, ga_ref, gb_ref, o_ref):
    a = _dot(oa_ref[...], _wtile(wa_ref))
    b = _dot(ob_ref[...], _wtile(wb_ref))
    o_ref[...] = (jax.nn.sigmoid(ga_ref[...]) * a + jax.nn.sigmoid(gb_ref[...]) * b).astype(o_ref.dtype)


def branch_merge(oa, wa, ob, wb, layer, proj, gate_col, tm, tn):
    m, ka = oa.shape
    kb = ob.shape[1]
    n = wa.shape[-1]
    ga0 = gate_col // tn
    gb0 = (gate_col + n) // tn
    return pl.pallas_call(
        _merge_kernel,
        grid=(m // tm, n // tn),
        in_specs=[pl.BlockSpec((tm, ka), lambda i, j: (i, 0)),
                  _wspec(wa, layer, tn),
                  pl.BlockSpec((tm, kb), lambda i, j: (i, 0)),
                  _wspec(wb, layer, tn),
                  pl.BlockSpec((tm, tn), lambda i, j: (i, ga0 + j)),
                  pl.BlockSpec((tm, tn), lambda i, j: (i, gb0 + j))],
        out_specs=pl.BlockSpec((tm, tn), lambda i, j: (i, j)),
        out_shape=jax.ShapeDtypeStruct((m, n), BF16),
        compiler_params=_cparams(("parallel", "arbitrary")),
        name="branch_merge",
    )(oa, wa, ob, wb, proj, proj)


def _alpha_kernel(x_ref, w1t_ref, w2_ref, b_ref, o_ref):
    a = lax.dot_general(x_ref[...], w1t_ref[...], NT_DIMS, preferred_element_type=F32)
    z = _dot(a.astype(BF16), w2_ref[...]) + b_ref[...]
    o_ref[...] = (jnp.minimum(z, 0.0) - jnp.log1p(jnp.exp(-jnp.abs(z)))) * (1.0 / GLA_TAU)


def gla_log_decay(xn, w1t, w2, b, tr):
    m, d = xn.shape
    kw = w2.shape[1]
    return pl.pallas_call(
        _alpha_kernel,
        grid=(m // tr,),
        in_specs=[pl.BlockSpec((tr, d), lambda i: (i, 0)),
                  pl.BlockSpec((LANE, d), lambda i: (0, 0)),
                  pl.BlockSpec((LANE, kw), lambda i: (0, 0)),
                  pl.BlockSpec((1, kw), lambda i: (0, 0))],
        out_specs=pl.BlockSpec((tr, kw), lambda i: (i, 0)),
        out_shape=jax.ShapeDtypeStruct((m, kw), F32),
        compiler_params=_cparams(("parallel",)),
        name="gla_log_decay",
    )(xn, w1t, w2, b.reshape(1, kw))


def _gla_kernel(q_ref, k_ref, v_ref, r_ref, lg_ref, s0_ref, g_ref, o_ref, sout_ref, s_scr,
                *, chunk, heads, n_valid):
    c = pl.program_id(2)

    @pl.when(c == 0)
    def _():
        s_scr[...] = s0_ref[...]

    valid = None
    if n_valid is not None:
        valid = c * chunk + lax.broadcasted_iota(jnp.int32, (chunk, 1), 0) < n_valid
    ri = lax.broadcasted_iota(jnp.int32, (chunk, chunk), 0)
    ci = lax.broadcasted_iota(jnp.int32, (chunk, chunk), 1)
    tril = (ci <= ri).astype(BF16)
    g = g_ref[...]

    for hh in range(heads):
        dk = slice(hh * GLA_DK, (hh + 1) * GLA_DK)
        dv = slice(hh * GLA_DV, (hh + 1) * GLA_DV)
        o, s_new = _gla_head(q_ref[:, dk], k_ref[:, dk], v_ref[:, dv], lg_ref[:, dk], s_scr[hh],
                             tril, valid, chunk)
        s_scr[hh] = s_new
        sout_ref[hh] = s_new
        ms = jnp.mean(o * o, axis=-1, keepdims=True)
        on = o * lax.rsqrt(ms + EPS) * g
        r = r_ref[:, dv]
        o_ref[:, dv] = (on * (r * jax.nn.sigmoid(r))).astype(o_ref.dtype)


def _gla_head(q, k, v, lg, st, tril, valid, chunk):
    q = q * (GLA_DK ** -0.5)
    if valid is not None:
        lg = jnp.where(valid, lg, 0.0)
        k = jnp.where(valid, k, 0.0)

    hi = lg.astype(BF16)
    r1 = lg - hi.astype(F32)
    mid = r1.astype(BF16)
    lo = (r1 - mid.astype(F32)).astype(BF16)
    b = _dot(tril, hi) + _dot(tril, mid) + _dot(tril, lo)

    vb = v.astype(BF16)
    o_inter = lax.dot_general((q * jnp.exp(b)).astype(BF16), st.astype(BF16), NT_DIMS,
                              preferred_element_type=F32)

    outs = []
    for i in range(chunk // GLA_SUB):
        lo_r = i * GLA_SUB
        hi_r = lo_r + GLA_SUB
        mid_r = lo_r + GLA_SUB // 2 - 1
        ref_b = b[mid_r:mid_r + 1, :]
        qs = (q[lo_r:hi_r] * jnp.exp(b[lo_r:hi_r] - ref_b)).astype(BF16)
        ks = (k[:hi_r] * jnp.exp(ref_b - b[:hi_r])).astype(BF16)
        a = lax.dot_general(qs, ks, NT_DIMS, preferred_element_type=F32)
        t_idx = lo_r + lax.broadcasted_iota(jnp.int32, (GLA_SUB, hi_r), 0)
        s_idx = lax.broadcasted_iota(jnp.int32, (GLA_SUB, hi_r), 1)
        a = jnp.where(s_idx <= t_idx, a, 0.0)
        outs.append(o_inter[lo_r:hi_r] + _dot(a.astype(BF16), vb[:hi_r]))
    o = jnp.concatenate(outs, axis=0)

    b_last = b[chunk - 1:chunk, :]
    kd = (k * jnp.exp(b_last - b)).astype(BF16)
    s_new = st * jnp.exp(b_last) + lax.dot_general(vb, kd, TN_DIMS, preferred_element_type=F32)
    return o, s_new


def gla(proj, lg, s0, g, *, row0, n_seq, n_chunks, chunk, n_valid, n_heads, k_col, v_col, r_col):
    hg = GLA_HEAD_GROUP
    wk, wv = hg * GLA_DK, hg * GLA_DV
    assert n_heads % hg == 0 and k_col % wk == 0 and v_col % wv == 0 and r_col % wv == 0
    base = row0 // chunk
    kc, vc, rc = k_col // wk, v_col // wv, r_col // wv
    rowblk = lambda s, c: base + s * n_chunks + c
    kern = functools.partial(_gla_kernel, chunk=chunk, heads=hg,
                             n_valid=None if n_valid == n_chunks * chunk else n_valid)
    state_spec = pl.BlockSpec((None, hg, GLA_DV, GLA_DK), lambda s, h, c: (s, h, 0, 0))
    return pl.pallas_call(
        kern,
        grid=(n_seq, n_heads // hg, n_chunks),
        in_specs=[pl.BlockSpec((chunk, wk), lambda s, h, c: (rowblk(s, c), h)),
                  pl.BlockSpec((chunk, wk), lambda s, h, c: (rowblk(s, c), kc + h)),
                  pl.BlockSpec((chunk, wv), lambda s, h, c: (rowblk(s, c), vc + h)),
                  pl.BlockSpec((chunk, wv), lambda s, h, c: (rowblk(s, c), rc + h)),
                  pl.BlockSpec((chunk, wk), lambda s, h, c: (rowblk(s, c), h)),
                  state_spec,
                  pl.BlockSpec((1, GLA_DV), lambda s, h, c: (0, 0))],
        out_specs=[pl.BlockSpec((chunk, wv), lambda s, h, c: (s * n_chunks + c, h)),
                   state_spec],
        out_shape=[jax.ShapeDtypeStruct((n_seq * n_chunks * chunk, n_heads * GLA_DV), BF16),
                   jax.ShapeDtypeStruct((n_seq, n_heads, GLA_DV, GLA_DK), F32)],
        scratch_shapes=[pltpu.VMEM((hg, GLA_DV, GLA_DK), F32)],
        compiler_params=_cparams(("parallel", "parallel", "arbitrary")),
        name="gla_chunk",
    )(proj, proj, proj, proj, lg, s0, g.reshape(1, GLA_DV))


def _qknorm_kernel(q_ref, k_ref, v_ref, qg_ref, kg_ref, ko_ref, vo_ref, qh_ref, kh_ref, vh_ref,
                   *, n_heads, tr):
    qg = qg_ref[...]
    kg = kg_ref[...]
    for h in range(n_heads):
        sl = slice(h * SB_DH, (h + 1) * SB_DH)
        q = q_ref[:, sl]
        k = k_ref[:, sl]
        v = v_ref[:, sl]
        qn = q * lax.rsqrt(jnp.mean(q * q, axis=-1, keepdims=True) + EPS) * qg
        kn = k * lax.rsqrt(jnp.mean(k * k, axis=-1, keepdims=True) + EPS) * kg
        ko_ref[pl.ds(h, tr, stride=n_heads), :] = kn
        vo_ref[pl.ds(h, tr, stride=n_heads), :] = v
        qh_ref[:, sl] = (qn * SB_SCALE).astype(BF16)
        kh_ref[:, sl] = kn.astype(BF16)
        vh_ref[:, sl] = v.astype(BF16)


def qk_norm(proj, qg, kg, q_col, sb_w, tr):
    m = proj.shape[0]
    n_heads = sb_w // SB_DH
    c0 = q_col // sb_w
    row = lambda i: (i, 0)
    out = jax.ShapeDtypeStruct((m, sb_w), BF16)
    cache = jax.ShapeDtypeStruct((m * n_heads, SB_DH), F32)
    return pl.pallas_call(
        functools.partial(_qknorm_kernel, n_heads=n_heads, tr=tr),
        grid=(m // tr,),
        in_specs=[pl.BlockSpec((tr, sb_w), lambda i: (i, c0)),
                  pl.BlockSpec((tr, sb_w), lambda i: (i, c0 + 1)),
                  pl.BlockSpec((tr, sb_w), lambda i: (i, c0 + 2)),
                  pl.BlockSpec((1, SB_DH), lambda i: (0, 0)),
                  pl.BlockSpec((1, SB_DH), lambda i: (0, 0))],
        out_specs=[pl.BlockSpec((tr * n_heads, SB_DH), row)] * 2 + [pl.BlockSpec((tr, sb_w), row)] * 3,
        out_shape=[cache, cache, out, out, out],
        compiler_params=_cparams(("parallel",)),
        name="qk_norm",
    )(proj, proj, proj, qg.reshape(1, SB_DH), kg.reshape(1, SB_DH))


def _sb_tile(q, kblk, vblk, trir, carry, vis):
    acc, rest = carry
    z = lax.dot_general(q, kblk, NT_DIMS, preferred_element_type=F32)
    sp = _softplus(z)
    if vis is not None:
        sp = jnp.where(vis, sp, 0.0)
    hi = sp.astype(BF16)
    lo = (sp - hi.astype(F32)).astype(BF16)
    cs = _dot(hi, trir) + _dot(lo, trir)
    w = jnp.exp(z - cs - rest)
    if vis is not None:
        w = jnp.where(vis, w, 0.0)
    acc = acc + _dot(w.astype(BF16), vblk)
    return acc, rest + cs[:, 0:1]


def _rev_tri(n):
    s = lax.broadcasted_iota(jnp.int32, (n, n), 0)
    r = lax.broadcasted_iota(jnp.int32, (n, n), 1)
    return (s >= r).astype(BF16)


def _strict_causal(tq, tk):
    t = lax.broadcasted_iota(jnp.int32, (tq, tk), 0)
    s = lax.broadcasted_iota(jnp.int32, (tq, tk), 1)
    return s < t


def _key_norm_max(k_ref, rows, step):
    def body(c, m):
        off = pl.multiple_of(c * step, step)
        kf = k_ref[pl.ds(off, step), :].astype(F32)
        return jnp.maximum(m, jnp.sum(kf * kf, axis=-1, keepdims=True))

    m = lax.fori_loop(0, rows // step, body, jnp.zeros((step, 1), F32))
    return jnp.sqrt(jnp.max(m))


def _sb_sweep(q, acc, rest, n_tiles, k_norm_max, load_tile, trir):
    qf = q.astype(F32)
    q_norm_max = jnp.sqrt(jnp.max(jnp.sum(qf * qf, axis=-1, keepdims=True)))
    z_bound = q_norm_max * k_norm_max * 1.01 + 1.0

    def live(rest):
        return jnp.logical_not(z_bound - jnp.min(rest) < SB_ZERO_EXP).astype(jnp.int32)

    def cond(c):
        return jnp.logical_and(c[0] >= 0, c[3] != 0)

    def body(c):
        j, acc, rest, _ = c
        kblk, vblk = load_tile(j)
        acc, rest = _sb_tile(q, kblk, vblk, trir, (acc, rest), None)
        return j - 1, acc, rest, live(rest)

    return lax.while_loop(cond, body, (n_tiles - 1, acc, rest, live(rest)))[1]


def _sb_prompt_kernel(q_ref, k_ref, v_ref, tri_ref, o_ref, kmax_ref, *, tile, seq_rows):
    i = pl.program_id(2)

    @pl.when(i == 0)
    def _():
        kmax_ref[0] = _key_norm_max(k_ref, seq_rows, tile)

    q = q_ref[...]
    trir = tri_ref[...]

    def load(j):
        off = pl.multiple_of(j * tile, tile)
        return k_ref[pl.ds(off, tile), :], v_ref[pl.ds(off, tile), :]

    carry = (jnp.zeros((tile, SB_DH), F32), jnp.zeros((tile, 1), F32))
    kd, vd = load(i)
    acc, rest = _sb_tile(q, kd, vd, trir, carry, _strict_causal(tile, tile))
    acc = _sb_sweep(q, acc, rest, i, kmax_ref[0], load, trir)
    o_ref[...] = acc.astype(o_ref.dtype)


def sb_prompt(qh, kh, vh, tri, *, n_seq, seq_rows, n_heads):
    tile = SB_TILE
    nqt = seq_rows // tile
    kv_spec = pl.BlockSpec((seq_rows, SB_DH), lambda s, h, i: (s, h))
    q_spec = pl.BlockSpec((tile, SB_DH), lambda s, h, i: (s * nqt + i, h))
    return pl.pallas_call(
        functools.partial(_sb_prompt_kernel, tile=tile, seq_rows=seq_rows),
        grid=(n_seq, n_heads, nqt),
        in_specs=[q_spec, kv_spec, kv_spec, pl.BlockSpec((tile, tile), lambda s, h, i: (0, 0))],
        out_specs=q_spec,
        out_shape=jax.ShapeDtypeStruct((n_seq * seq_rows, n_heads * SB_DH), BF16),
        scratch_shapes=[pltpu.SMEM((1,), F32)],
        compiler_params=_cparams(("parallel", "parallel", "arbitrary")),
        name="sb_prompt",
    )(qh, kh, vh, tri)


def _sb_sample_kernel(q_ref, kn_ref, vn_ref, kc_ref, vc_ref, tri_ref, o_ref, *, t_new, past, tile):
    q = q_ref[...]
    carry = (jnp.zeros((t_new, SB_DH), F32), jnp.zeros((t_new, 1), F32))
    acc, rest = _sb_tile(q, kn_ref[...], vn_ref[...], _rev_tri(t_new), carry, _strict_causal(t_new, t_new))

    def load(j):
        off = pl.multiple_of(j * tile, tile)
        return kc_ref[pl.ds(off, tile), :].astype(BF16), vc_ref[pl.ds(off, tile), :].astype(BF16)

    acc = _sb_sweep(q, acc, rest, past // tile, _key_norm_max(kc_ref, past, tile), load, tri_ref[...])
    o_ref[...] = acc.astype(o_ref.dtype)


def sb_sample(qh, kh, vh, cache_k, cache_v, tri, layer, *, row0, n_seq, t_new, n_heads):
    past = cache_k.shape[2]
    tile = SB_TILE
    assert past % tile == 0 and row0 % t_new == 0
    base = row0 // t_new
    new_spec = pl.BlockSpec((t_new, SB_DH), lambda s, h: (base + s, h))
    cache_spec = pl.BlockSpec((None, None, past, SB_DH), lambda s, h: (layer, s, 0, h))
    return pl.pallas_call(
        functools.partial(_sb_sample_kernel, t_new=t_new, past=past, tile=tile),
        grid=(n_seq, n_heads),
        in_specs=[new_spec, new_spec, new_spec, cache_spec, cache_spec,
                  pl.BlockSpec((tile, tile), lambda s, h: (0, 0))],
        out_specs=pl.BlockSpec((t_new, SB_DH), lambda s, h: (s, h)),
        out_shape=jax.ShapeDtypeStruct((n_seq * t_new, n_heads * SB_DH), BF16),
        compiler_params=_cparams(("parallel", "parallel")),
        name="sb_sample",
    )(qh, kh, vh, cache_k, cache_v, tri)


def _moe_up_kernel(te_ref, tv_ref, tf_ref, x_ref, wg_ref, wu_ref, o_ref, wg_s, wu_s):
    i = pl.program_id(1)

    @pl.when(tf_ref[i] != 0)
    def _():
        wg_s[...] = wg_ref[...].astype(BF16)
        wu_s[...] = wu_ref[...].astype(BF16)

    @pl.when(tv_ref[i] != 0)
    def _():
        x = x_ref[...]
        a = _dot(x, wg_s[...])
        u = _dot(x, wu_s[...])
        o_ref[...] = (a * jax.nn.sigmoid(a) * u).astype(o_ref.dtype)

    @pl.when(tv_ref[i] == 0)
    def _():
        o_ref[...] = jnp.zeros_like(o_ref)


def _moe_down_kernel(te_ref, tv_ref, tf_ref, x_ref, w_ref, o_ref, w_s):
    i = pl.program_id(1)

    @pl.when(tf_ref[i] != 0)
    def _():
        w_s[...] = w_ref[...].astype(BF16)

    @pl.when(tv_ref[i] != 0)
    def _():
        o_ref[...] = _dot(x_ref[...], w_s[...])

    @pl.when(tv_ref[i] == 0)
    def _():
        o_ref[...] = jnp.zeros_like(o_ref)


def _moe_call(kern, x, ws, layer, tiles, tm, tn, out_dtype, name):
    n_rows, k = x.shape
    n = ws[0].shape[-1]
    wspec = pl.BlockSpec((None, None, k, tn), lambda j, i, te, tv, tf: (layer, te[i], 0, j))
    return pl.pallas_call(
        kern,
        grid_spec=pltpu.PrefetchScalarGridSpec(
            num_scalar_prefetch=3,
            grid=(n // tn, n_rows // tm),
            in_specs=[pl.BlockSpec((tm, k), lambda j, i, te, tv, tf: (i, 0))] + [wspec] * len(ws),
            out_specs=pl.BlockSpec((tm, tn), lambda j, i, te, tv, tf: (i, j)),
            scratch_shapes=[pltpu.VMEM((k, tn), BF16)] * len(ws)),
        out_shape=jax.ShapeDtypeStruct((n_rows, n), out_dtype),
        compiler_params=_cparams(("arbitrary", "arbitrary")),
        name=name,
    )(*tiles, x, *ws)


def moe_ffn(h, hn, top_idx, top_w, wg, wu, wd, layer, out_rows=None):
    m = hn.shape[0]
    n_assign = 2 * m
    tm = MOE_TILE
    n_rows = _round_up(n_assign + N_EXPERTS * (tm - 1), tm)
    n_tiles = n_rows // tm
    e_flat = top_idx.reshape(-1)
    order = jnp.argsort(e_flat, stable=True)
    e_sorted = e_flat[order]
    counts = jnp.zeros(N_EXPERTS, jnp.int32).at[e_flat].add(1)
    padded = (counts + tm - 1) // tm * tm
    start = jnp.cumsum(counts) - counts
    pend = jnp.cumsum(padded)
    pstart = pend - padded
    dest = (pstart[e_sorted] + (jnp.arange(n_assign, dtype=jnp.int32) - start[e_sorted])).astype(jnp.int32)
    row_tok = jnp.zeros(n_rows, jnp.int32).at[dest].set((order // 2).astype(jnp.int32))
    pos = jnp.zeros(n_assign, jnp.int32).at[order].set(dest).reshape(m, 2)
    tile_start = jnp.arange(n_tiles, dtype=jnp.int32) * tm
    tile_valid = (tile_start < pend[-1]).astype(jnp.int32)
    last_valid = jnp.maximum(pend[-1] // tm - 1, 0)
    tile_expert = jnp.clip(jnp.searchsorted(pend, tile_start, side="right"), 0, N_EXPERTS - 1).astype(jnp.int32)
    tile_expert = jnp.where(tile_valid != 0, tile_expert, tile_expert[last_valid])
    tile_first = jnp.concatenate([jnp.ones((1,), jnp.int32),
                                  (tile_expert[1:] != tile_expert[:-1]).astype(jnp.int32)])
    tiles = (tile_expert, tile_valid, tile_first)

    xg = hn[row_tok]
    hg = _moe_call(_moe_up_kernel, xg, (wg, wu), layer, tiles, tm, MOE_UP_TN, BF16, "moe_up")
    yb = _moe_call(_moe_down_kernel, hg, (wd,), layer, tiles, tm, MOE_DOWN_TN, F32, "moe_down")

    def combine(lo, hi):
        return h[lo:hi] + (yb[pos[lo:hi, 0]] * top_w[lo:hi, 0:1] + yb[pos[lo:hi, 1]] * top_w[lo:hi, 1:2])

    if out_rows is None:
        return combine(0, m)
    return [combine(lo, hi) for lo, hi in out_rows]


def kernel(x_prompt, x_sample, cache_k_sb, cache_v_sb, state_gla, meta_tokens, norm_mix, norm_ffn, w_in, w_alpha2, b_alpha, gla_norm, q_norm, k_norm, w_branch_a, w_branch_b, w_out, w_ff_gate, w_ff_up, w_ff_down, w_router, w_moe_gate, w_moe_up, w_moe_down):
    batch, seq, d_model = x_prompt.shape
    dec_batch, dec_seq, _ = x_sample.shape
    depth = w_in.shape[0]
    n_meta = meta_tokens.shape[0]
    past = cache_k_sb.shape[2]
    gla_rank, gla_kw = w_alpha2.shape[1], w_alpha2.shape[2]
    gla_w = w_branch_a.shape[1]
    sb_w = w_branch_b.shape[1]
    gla_heads = gla_w // GLA_DV
    sb_heads = sb_w // SB_DH
    d_ff = w_ff_gate.shape[2]
    assert gla_kw == gla_heads * GLA_DK and dec_seq % GLA_SUB == 0

    alr_col = 2 * gla_kw + 2 * gla_w
    k_col, v_col, r_col = gla_kw, 2 * gla_kw, 2 * gla_kw + gla_w
    gate_col = 3 * sb_w

    lp = n_meta + seq
    lpad = _round_up(lp, ROW_ALIGN)
    m_prompt = batch * lpad
    m_sample = dec_batch * dec_seq
    m_tot = m_prompt + m_sample
    tm = _tile(m_tot, 1024)
    tr = _tile(m_tot, 256)
    tn = 512

    pieces = []
    for b in range(batch):
        pieces += [meta_tokens.astype(F32), x_prompt[b], jnp.zeros((lpad - lp, d_model), F32)]
    h = jnp.concatenate(pieces + [x_sample.reshape(m_sample, d_model)], axis=0)
    out_rows = [(b * lpad + n_meta, b * lpad + lp) for b in range(batch)] + [(m_prompt, m_tot)]

    cache_k = cache_k_sb.reshape(depth, dec_batch, past, sb_w)
    cache_v = cache_v_sb.reshape(depth, dec_batch, past, sb_w)
    s0_prompt = jnp.zeros((batch, gla_heads, GLA_DV, GLA_DK), F32)

    d_ffp = _round_up(d_ff, D_FF_ALIGN)

    lpa = _round_up(lpad, SB_TILE)
    tri = jnp.tril(jnp.ones((SB_TILE, SB_TILE), BF16))

    def attn_rows(a):
        p = a[:m_prompt].reshape(batch, lpad, sb_w)
        return jnp.pad(p, ((0, 0), (0, lpa - lpad), (0, 0))).reshape(batch * lpa, sb_w)

    w_in_t = jnp.swapaxes(w_in, 1, 2)

    kp, vp, sp, ks_, vs_, ss_ = [], [], [], [], [], []
    for l in range(depth):
        w_bt = w_in_t[l, alr_col + gla_rank:, :].astype(BF16)
        w_alr_t = jnp.pad(w_in_t[l, alr_col:alr_col + gla_rank, :], ((0, LANE - gla_rank), (0, 0))).astype(BF16)
        w_a2 = jnp.pad(w_alpha2[l], ((0, LANE - gla_rank), (0, 0))).astype(BF16)

        xn = rmsnorm(h, norm_mix[l], tr)
        proj_a = matmul_nt(xn, w_in_t, tm, tn, F32, layer=l, n=alr_col, name="in_proj_gla")
        proj_b = matmul_nt(xn, w_bt, tm, tn, F32, name="in_proj_sb")
        lg = gla_log_decay(xn, w_alr_t, w_a2, b_alpha[l], tm)

        gla_cols = dict(n_heads=gla_heads, k_col=k_col, v_col=v_col, r_col=r_col)
        oa_p, s_p = gla(proj_a, lg, s0_prompt, gla_norm[l], row0=0, n_seq=batch, n_chunks=lpad // ROW_ALIGN,
                        chunk=ROW_ALIGN, n_valid=lp, **gla_cols)
        oa_s, s_s = gla(proj_a, lg, jnp.swapaxes(state_gla[l], -1, -2), gla_norm[l], row0=m_prompt,
                        n_seq=dec_batch, n_chunks=1, chunk=dec_seq, n_valid=dec_seq, **gla_cols)
        oa = jnp.concatenate([oa_p, oa_s], axis=0)

        k_out, v_out, qh, kh, vh = qk_norm(proj_b, q_norm[l], k_norm[l], 0, sb_w, tr)
        ob_p = sb_prompt(attn_rows(qh), attn_rows(kh), attn_rows(vh), tri, n_seq=batch, seq_rows=lpa,
                         n_heads=sb_heads)
        ob_p = ob_p.reshape(batch, lpa, sb_w)[:, :lpad].reshape(m_prompt, sb_w)
        ob_s = sb_sample(qh, kh, vh, cache_k, cache_v, tri, l, row0=m_prompt, n_seq=dec_batch, t_new=dec_seq,
                         n_heads=sb_heads)
        ob = jnp.concatenate([ob_p, ob_s], axis=0)

        merged = branch_merge(oa, w_branch_a, ob, w_branch_b, l, proj_b, gate_col, tm, tn)
        h = matmul(merged, w_out, tm, tn, F32, res=h, layer=l, name="out_proj")

        prompt_rows = lambda a: a[:m_prompt * sb_heads].reshape(batch, lpad, sb_heads, SB_DH)[:, :lp]
        sample_rows = lambda a: a[m_prompt * sb_heads:].reshape(dec_batch, dec_seq, sb_heads, SB_DH)
        kp.append(prompt_rows(k_out))
        vp.append(prompt_rows(v_out))
        sp.append(jnp.swapaxes(s_p, -1, -2))
        ks_.append(sample_rows(k_out))
        vs_.append(sample_rows(v_out))
        ss_.append(jnp.swapaxes(s_s, -1, -2))

        i = l // 2
        if l % 2 == 0:
            hn = rmsnorm(h, norm_ffn[l], tr)
            pad_c = ((0, 0), (0, d_ffp - d_ff))
            wg = jnp.pad(w_ff_gate[i], pad_c).astype(BF16)
            wu = jnp.pad(w_ff_up[i], pad_c).astype(BF16)
            wd = jnp.pad(w_ff_down[i], ((0, d_ffp - d_ff), (0, 0))).astype(BF16)
            mid = swiglu_up(hn, wg, wu, tm, tn)
            h = matmul_acc_res(mid, wd, h, tm, _tile(d_model, 1024), _tile(d_ffp, 3072), name="ffn_down")
        else:
            hn, top_idx, top_w = rmsnorm_router(h, norm_ffn[l], w_router[i], tr)
            h = moe_ffn(h, hn, top_idx[:, :2], top_w[:, :2], w_moe_gate, w_moe_up, w_moe_down, i,
                        out_rows=out_rows if l == depth - 1 else None)

    outs = h if isinstance(h, list) else [h[lo:hi] for lo, hi in out_rows]
    y_prompt = jnp.stack(outs[:batch])
    y_sample = outs[batch].reshape(dec_batch, dec_seq, d_model)
    return (y_prompt, y_sample, jnp.stack(kp), jnp.stack(vp), jnp.stack(sp),
            jnp.stack(ks_), jnp.stack(vs_), jnp.stack(ss_))
```

```python
import functools

import jax
import jax.numpy as jnp
from jax import lax
from jax.experimental import pallas as pl
from jax.experimental.pallas import tpu as pltpu

F32 = jnp.float32
BF16 = jnp.bfloat16

GLA_DK = 128
GLA_DV = 256
GLA_TAU = 16.0
SB_DH = 128
SB_SCALE = SB_DH ** -0.5
N_EXPERTS = 8
EPS = 1e-6

LANE = 128
SUBLANE = 8
V7X_VMEM_BYTES = 64 * 1024 * 1024
VMEM_LIMIT = V7X_VMEM_BYTES - 8 * 1024 * 1024

ROW_ALIGN = 128
GLA_SUB = 32
GLA_HEAD_GROUP = 12
SB_TILE = 256
SB_ZERO_EXP = -110.0
MOE_TILE = 512
MOE_UP_TN = 512
MOE_DOWN_TN = 512
D_FF_ALIGN = 1024

NT_DIMS = (((1,), (1,)), ((), ()))
TN_DIMS = (((0,), (0,)), ((), ()))


def _cparams(sem):
    return pltpu.CompilerParams(dimension_semantics=sem, vmem_limit_bytes=VMEM_LIMIT)


def _round_up(x, m):
    return (x + m - 1) // m * m


def _tile(n, cap, align=LANE):
    best = None
    t = align
    while t <= min(n, cap):
        if n % t == 0:
            best = t
        t += align
    assert best is not None, (n, cap, align)
    return best


def _dot(a, b):
    return jnp.dot(a, b, preferred_element_type=F32)


def _softplus(z):
    return jnp.maximum(z, 0.0) + jnp.log1p(jnp.exp(-jnp.abs(z)))


def _rmsnorm_kernel(x_ref, g_ref, o_ref):
    x = x_ref[...]
    ms = jnp.mean(x * x, axis=-1, keepdims=True)
    o_ref[...] = (x * lax.rsqrt(ms + EPS) * g_ref[...]).astype(o_ref.dtype)


def rmsnorm(x, g, tr):
    m, d = x.shape
    return pl.pallas_call(
        _rmsnorm_kernel,
        grid=(m // tr,),
        in_specs=[pl.BlockSpec((tr, d), lambda i: (i, 0)),
                  pl.BlockSpec((1, d), lambda i: (0, 0))],
        out_specs=pl.BlockSpec((tr, d), lambda i: (i, 0)),
        out_shape=jax.ShapeDtypeStruct((m, d), BF16),
        compiler_params=_cparams(("parallel",)),
        name="rmsnorm",
    )(x, g.reshape(1, d))


def _rmsnorm_router_kernel(x_ref, g_ref, wr_ref, o_ref, idx_ref, wgt_ref):
    x = x_ref[...]
    ms = jnp.mean(x * x, axis=-1, keepdims=True)
    xn = x * lax.rsqrt(ms + EPS) * g_ref[...]
    o_ref[...] = xn.astype(o_ref.dtype)
    logits = jnp.dot(xn, wr_ref[...], preferred_element_type=F32,
                     precision=lax.Precision.HIGHEST)
    lane = lax.broadcasted_iota(jnp.int32, logits.shape, 1).astype(F32)
    neg = jnp.float32(-jnp.inf)
    l0 = jnp.where(lane < N_EXPERTS, logits, neg)
    m0 = jnp.max(l0, axis=-1, keepdims=True)
    i0 = jnp.min(jnp.where(l0 == m0, lane, float(LANE)), axis=-1, keepdims=True)
    l1 = jnp.where(lane == i0, neg, l0)
    m1 = jnp.max(l1, axis=-1, keepdims=True)
    i1 = jnp.min(jnp.where(l1 == m1, lane, float(LANE)), axis=-1, keepdims=True)
    e1 = jnp.exp(m1 - m0)
    den = 1.0 + e1
    idx_ref[...] = jnp.where(lane == 0.0, i0, i1).astype(jnp.int32)
    wgt_ref[...] = jnp.where(lane == 0.0, 1.0 / den, e1 / den)


def rmsnorm_router(x, g, w_router, tr):
    m, d = x.shape
    wr = jnp.pad(w_router.astype(F32), ((0, 0), (0, LANE - w_router.shape[1])))
    row = lambda i: (i, 0)
    return pl.pallas_call(
        _rmsnorm_router_kernel,
        grid=(m // tr,),
        in_specs=[pl.BlockSpec((tr, d), row),
                  pl.BlockSpec((1, d), lambda i: (0, 0)),
                  pl.BlockSpec((d, LANE), lambda i: (0, 0))],
        out_specs=[pl.BlockSpec((tr, d), row),
                   pl.BlockSpec((tr, LANE), row),
                   pl.BlockSpec((tr, LANE), row)],
        out_shape=[jax.ShapeDtypeStruct((m, d), BF16),
                   jax.ShapeDtypeStruct((m, LANE), jnp.int32),
                   jax.ShapeDtypeStruct((m, LANE), F32)],
        compiler_params=_cparams(("parallel",)),
        name="rmsnorm_router",
    )(x, g.reshape(1, d), wr)


def _wtile(w_ref):
    return w_ref[...].astype(BF16)


def _wspec(w, layer, tn):
    if w.ndim == 2:
        return pl.BlockSpec((w.shape[0], tn), lambda i, j: (0, j))
    return pl.BlockSpec((None, w.shape[1], tn), lambda i, j: (layer, 0, j))


def _mm_kernel(x_ref, w_ref, o_ref):
    o_ref[...] = _dot(x_ref[...], _wtile(w_ref)).astype(o_ref.dtype)


def _mm_nt_kernel(x_ref, wt_ref, o_ref):
    wt = (wt_ref[0] if len(wt_ref.shape) == 3 else wt_ref[...]).astype(BF16)
    o_ref[...] = lax.dot_general(x_ref[...], wt, NT_DIMS, preferred_element_type=F32).astype(o_ref.dtype)


def _mm_res_kernel(x_ref, w_ref, r_ref, o_ref):
    o_ref[...] = (r_ref[...] + _dot(x_ref[...], _wtile(w_ref))).astype(o_ref.dtype)


def matmul_nt(x, wt, tm, tn, out_dtype, layer=None, row0=0, n=None, name="matmul_nt"):
    m, k = x.shape
    n = wt.shape[-2] - row0 if n is None else n
    assert row0 % SUBLANE == 0 and tn % SUBLANE == 0
    row = lambda j: (row0 // SUBLANE + j * (tn // SUBLANE)) * SUBLANE
    if wt.ndim == 2:
        wspec = pl.BlockSpec((pl.Element(tn), pl.Element(k)), lambda i, j: (row(j), 0))
    else:
        wspec = pl.BlockSpec((pl.Element(1), pl.Element(tn), pl.Element(k)),
                             lambda i, j: (layer, row(j), 0))
    return pl.pallas_call(
        _mm_nt_kernel,
        grid=(m // tm, n // tn),
        in_specs=[pl.BlockSpec((tm, k), lambda i, j: (i, 0)), wspec],
        out_specs=pl.BlockSpec((tm, tn), lambda i, j: (i, j)),
        out_shape=jax.ShapeDtypeStruct((m, n), out_dtype),
        compiler_params=_cparams(("parallel", "arbitrary")),
        name=name,
    )(x, wt)


def matmul(x, w, tm, tn, out_dtype, res=None, layer=None, n=None, name="matmul"):
    m, k = x.shape
    n = w.shape[-1] if n is None else n
    in_specs = [pl.BlockSpec((tm, k), lambda i, j: (i, 0)), _wspec(w, layer, tn)]
    args = [x, w]
    kern = _mm_kernel
    if res is not None:
        in_specs.append(pl.BlockSpec((tm, tn), lambda i, j: (i, j)))
        args.append(res)
        kern = _mm_res_kernel
    return pl.pallas_call(
        kern,
        grid=(m // tm, n // tn),
        in_specs=in_specs,
        out_specs=pl.BlockSpec((tm, tn), lambda i, j: (i, j)),
        out_shape=jax.ShapeDtypeStruct((m, n), out_dtype),
        compiler_params=_cparams(("parallel", "arbitrary")),
        name=name,
    )(*args)


def _mm_acc_res_kernel(x_ref, w_ref, r_ref, o_ref, acc_ref):
    kk = pl.program_id(2)

    @pl.when(kk == 0)
    def _():
        acc_ref[...] = jnp.zeros_like(acc_ref)

    acc_ref[...] += _dot(x_ref[...], w_ref[...])

    @pl.when(kk == pl.num_programs(2) - 1)
    def _():
        o_ref[...] = r_ref[...] + acc_ref[...]


def matmul_acc_res(x, w, res, tm, tn, tk, name="matmul_acc"):
    m, k = x.shape
    n = w.shape[1]
    return pl.pallas_call(
        _mm_acc_res_kernel,
        grid=(m // tm, n // tn, k // tk),
        in_specs=[pl.BlockSpec((tm, tk), lambda i, j, kk: (i, kk)),
                  pl.BlockSpec((tk, tn), lambda i, j, kk: (kk, j)),
                  pl.BlockSpec((tm, tn), lambda i, j, kk: (i, j))],
        out_specs=pl.BlockSpec((tm, tn), lambda i, j, kk: (i, j)),
        out_shape=jax.ShapeDtypeStruct((m, n), F32),
        scratch_shapes=[pltpu.VMEM((tm, tn), F32)],
        compiler_params=_cparams(("parallel", "arbitrary", "arbitrary")),
        name=name,
    )(x, w, res)


def _swiglu_kernel(x_ref, wg_ref, wu_ref, o_ref):
    x = x_ref[...]
    a = _dot(x, wg_ref[...])
    u = _dot(x, wu_ref[...])
    o_ref[...] = (a * jax.nn.sigmoid(a) * u).astype(o_ref.dtype)


def swiglu_up(x, wg, wu, tm, tn):
    m, k = x.shape
    n = wg.shape[1]
    wspec = pl.BlockSpec((k, tn), lambda i, j: (0, j))
    return pl.pallas_call(
        _swiglu_kernel,
        grid=(m // tm, n // tn),
        in_specs=[pl.BlockSpec((tm, k), lambda i, j: (i, 0)), wspec, wspec],
        out_specs=pl.BlockSpec((tm, tn), lambda i, j: (i, j)),
        out_shape=jax.ShapeDtypeStruct((m, n), BF16),
        compiler_params=_cparams(("parallel", "arbitrary")),
        name="swiglu_up",
    )(x, wg, wu)


def _merge_kernel(oa_ref, wa_ref, ob_ref, wb_ref, ga_ref, gb_ref, o_ref):
    a = _dot(oa_ref[...], _wtile(wa_ref))
    b = _dot(ob_ref[...], _wtile(wb_ref))
    o_ref[...] = (jax.nn.sigmoid(ga_ref[...]) * a + jax.nn.sigmoid(gb_ref[...]) * b).astype(o_ref.dtype)


def branch_merge(oa, wa, ob, wb, layer, proj, gate_col, tm, tn):
    m, ka = oa.shape
    kb = ob.shape[1]
    n = wa.shape[-1]
    ga0 = gate_col // tn
    gb0 = (gate_col + n) // tn
    return pl.pallas_call(
        _merge_kernel,
        grid=(m // tm, n // tn),
        in_specs=[pl.BlockSpec((tm, ka), lambda i, j: (i, 0)),
                  _wspec(wa, layer, tn),
                  pl.BlockSpec((tm, kb), lambda i, j: (i, 0)),
                  _wspec(wb, layer, tn),
                  pl.BlockSpec((tm, tn), lambda i, j: (i, ga0 + j)),
                  pl.BlockSpec((tm, tn), lambda i, j: (i, gb0 + j))],
        out_specs=pl.BlockSpec((tm, tn), lambda i, j: (i, j)),
        out_shape=jax.ShapeDtypeStruct((m, n), BF16),
        compiler_params=_cparams(("parallel", "arbitrary")),
        name="branch_merge",
    )(oa, wa, ob, wb, proj, proj)


def _alpha_kernel(x_ref, w1t_ref, w2_ref, b_ref, o_ref):
    a = lax.dot_general(x_ref[...], w1t_ref[...], NT_DIMS, preferred_element_type=F32)
    z = _dot(a.astype(BF16), w2_ref[...]) + b_ref[...]
    o_ref[...] = (jnp.minimum(z, 0.0) - jnp.log1p(jnp.exp(-jnp.abs(z)))) * (1.0 / GLA_TAU)


def gla_log_decay(xn, w1t, w2, b, tr):
    m, d = xn.shape
    kw = w2.shape[1]
    return pl.pallas_call(
        _alpha_kernel,
        grid=(m // tr,),
        in_specs=[pl.BlockSpec((tr, d), lambda i: (i, 0)),
                  pl.BlockSpec((LANE, d), lambda i: (0, 0)),
                  pl.BlockSpec((LANE, kw), lambda i: (0, 0)),
                  pl.BlockSpec((1, kw), lambda i: (0, 0))],
        out_specs=pl.BlockSpec((tr, kw), lambda i: (i, 0)),
        out_shape=jax.ShapeDtypeStruct((m, kw), F32),
        compiler_params=_cparams(("parallel",)),
        name="gla_log_decay",
    )(xn, w1t, w2, b.reshape(1, kw))


def _gla_kernel(q_ref, k_ref, v_ref, r_ref, lg_ref, s0_ref, g_ref, o_ref, sout_ref, s_scr,
                *, chunk, heads, n_valid):
    c = pl.program_id(2)

    @pl.when(c == 0)
    def _():
        s_scr[...] = s0_ref[...]

    valid = None
    if n_valid is not None:
        valid = c * chunk + lax.broadcasted_iota(jnp.int32, (chunk, 1), 0) < n_valid
    ri = lax.broadcasted_iota(jnp.int32, (chunk, chunk), 0)
    ci = lax.broadcasted_iota(jnp.int32, (chunk, chunk), 1)
    tril = (ci <= ri).astype(BF16)
    g = g_ref[...]

    for hh in range(heads):
        dk = slice(hh * GLA_DK, (hh + 1) * GLA_DK)
        dv = slice(hh * GLA_DV, (hh + 1) * GLA_DV)
        o, s_new = _gla_head(q_ref[:, dk], k_ref[:, dk], v_ref[:, dv], lg_ref[:, dk], s_scr[hh],
                             tril, valid, chunk)
        s_scr[hh] = s_new
        sout_ref[hh] = s_new
        ms = jnp.mean(o * o, axis=-1, keepdims=True)
        on = o * lax.rsqrt(ms + EPS) * g
        r = r_ref[:, dv]
        o_ref[:, dv] = (on * (r * jax.nn.sigmoid(r))).astype(o_ref.dtype)


def _gla_head(q, k, v, lg, st, tril, valid, chunk):
    q = q * (GLA_DK ** -0.5)
    if valid is not None:
        lg = jnp.where(valid, lg, 0.0)
        k = jnp.where(valid, k, 0.0)

    hi = lg.astype(BF16)
    r1 = lg - hi.astype(F32)
    mid = r1.astype(BF16)
    lo = (r1 - mid.astype(F32)).astype(BF16)
    b = _dot(tril, hi) + _dot(tril, mid) + _dot(tril, lo)

    vb = v.astype(BF16)
    o_inter = lax.dot_general((q * jnp.exp(b)).astype(BF16), st.astype(BF16), NT_DIMS,
                              preferred_element_type=F32)

    outs = []
    for i in range(chunk // GLA_SUB):
        lo_r = i * GLA_SUB
        hi_r = lo_r + GLA_SUB
        mid_r = lo_r + GLA_SUB // 2 - 1
        ref_b = b[mid_r:mid_r + 1, :]
        qs = (q[lo_r:hi_r] * jnp.exp(b[lo_r:hi_r] - ref_b)).astype(BF16)
        ks = (k[:hi_r] * jnp.exp(ref_b - b[:hi_r])).astype(BF16)
        a = lax.dot_general(qs, ks, NT_DIMS, preferred_element_type=F32)
        t_idx = lo_r + lax.broadcasted_iota(jnp.int32, (GLA_SUB, hi_r), 0)
        s_idx = lax.broadcasted_iota(jnp.int32, (GLA_SUB, hi_r), 1)
        a = jnp.where(s_idx <= t_idx, a, 0.0)
        outs.append(o_inter[lo_r:hi_r] + _dot(a.astype(BF16), vb[:hi_r]))
    o = jnp.concatenate(outs, axis=0)

    b_last = b[chunk - 1:chunk, :]
    kd = (k * jnp.exp(b_last - b)).astype(BF16)
    s_new = st * jnp.exp(b_last) + lax.dot_general(vb, kd, TN_DIMS, preferred_element_type=F32)
    return o, s_new


def gla(proj, lg, s0, g, *, row0, n_seq, n_chunks, chunk, n_valid, n_heads, k_col, v_col, r_col):
    hg = GLA_HEAD_GROUP
    wk, wv = hg * GLA_DK, hg * GLA_DV
    assert n_heads % hg == 0 and k_col % wk == 0 and v_col % wv == 0 and r_col % wv == 0
    base = row0 // chunk
    kc, vc, rc = k_col // wk, v_col // wv, r_col // wv
    rowblk = lambda s, c: base + s * n_chunks + c
    kern = functools.partial(_gla_kernel, chunk=chunk, heads=hg,
                             n_valid=None if n_valid == n_chunks * chunk else n_valid)
    state_spec = pl.BlockSpec((None, hg, GLA_DV, GLA_DK), lambda s, h, c: (s, h, 0, 0))
    return pl.pallas_call(
        kern,
        grid=(n_seq, n_heads // hg, n_chunks),
        in_specs=[pl.BlockSpec((chunk, wk), lambda s, h, c: (rowblk(s, c), h)),
                  pl.BlockSpec((chunk, wk), lambda s, h, c: (rowblk(s, c), kc + h)),
                  pl.BlockSpec((chunk, wv), lambda s, h, c: (rowblk(s, c), vc + h)),
                  pl.BlockSpec((chunk, wv), lambda s, h, c: (rowblk(s, c), rc + h)),
                  pl.BlockSpec((chunk, wk), lambda s, h, c: (rowblk(s, c), h)),
                  state_spec,
                  pl.BlockSpec((1, GLA_DV), lambda s, h, c: (0, 0))],
        out_specs=[pl.BlockSpec((chunk, wv), lambda s, h, c: (s * n_chunks + c, h)),
                   state_spec],
        out_shape=[jax.ShapeDtypeStruct((n_seq * n_chunks * chunk, n_heads * GLA_DV), BF16),
                   jax.ShapeDtypeStruct((n_seq, n_heads, GLA_DV, GLA_DK), F32)],
        scratch_shapes=[pltpu.VMEM((hg, GLA_DV, GLA_DK), F32)],
        compiler_params=_cparams(("parallel", "parallel", "arbitrary")),
        name="gla_chunk",
    )(proj, proj, proj, proj, lg, s0, g.reshape(1, GLA_DV))


def _qknorm_kernel(q_ref, k_ref, v_ref, qg_ref, kg_ref, ko_ref, vo_ref, qh_ref, kh_ref, vh_ref,
                   *, n_heads, tr):
    qg = qg_ref[...]
    kg = kg_ref[...]
    for h in range(n_heads):
        sl = slice(h * SB_DH, (h + 1) * SB_DH)
        q = q_ref[:, sl]
        k = k_ref[:, sl]
        v = v_ref[:, sl]
        qn = q * lax.rsqrt(jnp.mean(q * q, axis=-1, keepdims=True) + EPS) * qg
        kn = k * lax.rsqrt(jnp.mean(k * k, axis=-1, keepdims=True) + EPS) * kg
        ko_ref[pl.ds(h, tr, stride=n_heads), :] = kn
        vo_ref[pl.ds(h, tr, stride=n_heads), :] = v
        qh_ref[:, sl] = (qn * SB_SCALE).astype(BF16)
        kh_ref[:, sl] = kn.astype(BF16)
        vh_ref[:, sl] = v.astype(BF16)


def qk_norm(proj, qg, kg, q_col, sb_w, tr):
    m = proj.shape[0]
    n_heads = sb_w // SB_DH
    c0 = q_col // sb_w
    row = lambda i: (i, 0)
    out = jax.ShapeDtypeStruct((m, sb_w), BF16)
    cache = jax.ShapeDtypeStruct((m * n_heads, SB_DH), F32)
    return pl.pallas_call(
        functools.partial(_qknorm_kernel, n_heads=n_heads, tr=tr),
        grid=(m // tr,),
        in_specs=[pl.BlockSpec((tr, sb_w), lambda i: (i, c0)),
                  pl.BlockSpec((tr, sb_w), lambda i: (i, c0 + 1)),
                  pl.BlockSpec((tr, sb_w), lambda i: (i, c0 + 2)),
                  pl.BlockSpec((1, SB_DH), lambda i: (0, 0)),
                  pl.BlockSpec((1, SB_DH), lambda i: (0, 0))],
        out_specs=[pl.BlockSpec((tr * n_heads, SB_DH), row)] * 2 + [pl.BlockSpec((tr, sb_w), row)] * 3,
        out_shape=[cache, cache, out, out, out],
        compiler_params=_cparams(("parallel",)),
        name="qk_norm",
    )(proj, proj, proj, qg.reshape(1, SB_DH), kg.reshape(1, SB_DH))


def _sb_tile(q, kblk, vblk, trir, carry, vis):
    acc, rest = carry
    z = lax.dot_general(q, kblk, NT_DIMS, preferred_element_type=F32)
    sp = _softplus(z)
    if vis is not None:
        sp = jnp.where(vis, sp, 0.0)
    hi = sp.astype(BF16)
    lo = (sp - hi.astype(F32)).astype(BF16)
    cs = _dot(hi, trir) + _dot(lo, trir)
    w = jnp.exp(z - cs - rest)
    if vis is not None:
        w = jnp.where(vis, w, 0.0)
    acc = acc + _dot(w.astype(BF16), vblk)
    return acc, rest + cs[:, 0:1]


def _rev_tri(n):
    s = lax.broadcasted_iota(jnp.int32, (n, n), 0)
    r = lax.broadcasted_iota(jnp.int32, (n, n), 1)
    return (s >= r).astype(BF16)


def _strict_causal(tq, tk):
    t = lax.broadcasted_iota(jnp.int32, (tq, tk), 0)
    s = lax.broadcasted_iota(jnp.int32, (tq, tk), 1)
    return s < t


def _key_norm_max(k_ref, rows, step, head=0, n_heads=1):
    def body(c, m):
        base = pl.multiple_of(c * (step * n_heads), step * n_heads)
        if n_heads == 1:
            kf = k_ref[pl.ds(base, step), :].astype(F32)
        else:
            kf = k_ref[pl.ds(base + head, step, stride=n_heads), :].astype(F32)
        return jnp.maximum(m, jnp.sum(kf * kf, axis=-1, keepdims=True))

    m = lax.fori_loop(0, rows // step, body, jnp.zeros((step, 1), F32))
    return jnp.sqrt(jnp.max(m))


def _sb_sweep(q, acc, rest, n_tiles, k_norm_max, load_tile, trir):
    qf = q.astype(F32)
    q_norm_max = jnp.sqrt(jnp.max(jnp.sum(qf * qf, axis=-1, keepdims=True)))
    z_bound = q_norm_max * k_norm_max * 1.01 + 1.0

    def live(rest):
        return jnp.logical_not(z_bound - jnp.min(rest) < SB_ZERO_EXP).astype(jnp.int32)

    def cond(c):
        return jnp.logical_and(c[0] >= 0, c[3] != 0)

    def body(c):
        j, acc, rest, _ = c
        kblk, vblk = load_tile(j)
        acc, rest = _sb_tile(q, kblk, vblk, trir, (acc, rest), None)
        return j - 1, acc, rest, live(rest)

    return lax.while_loop(cond, body, (n_tiles - 1, acc, rest, live(rest)))[1]


def _sb_prompt_kernel(q_ref, k_ref, v_ref, tri_ref, o_ref, kmax_ref, *, tile, seq_rows):
    i = pl.program_id(2)

    @pl.when(i == 0)
    def _():
        kmax_ref[0] = _key_norm_max(k_ref, seq_rows, tile)

    q = q_ref[...]
    trir = tri_ref[...]

    def load(j):
        off = pl.multiple_of(j * tile, tile)
        return k_ref[pl.ds(off, tile), :], v_ref[pl.ds(off, tile), :]

    carry = (jnp.zeros((tile, SB_DH), F32), jnp.zeros((tile, 1), F32))
    kd, vd = load(i)
    acc, rest = _sb_tile(q, kd, vd, trir, carry, _strict_causal(tile, tile))
    acc = _sb_sweep(q, acc, rest, i, kmax_ref[0], load, trir)
    o_ref[...] = acc.astype(o_ref.dtype)


def sb_prompt(qh, kh, vh, tri, *, n_seq, seq_rows, n_heads):
    tile = SB_TILE
    nqt = seq_rows // tile
    kv_spec = pl.BlockSpec((seq_rows, SB_DH), lambda s, h, i: (s, h))
    q_spec = pl.BlockSpec((tile, SB_DH), lambda s, h, i: (s * nqt + i, h))
    return pl.pallas_call(
        functools.partial(_sb_prompt_kernel, tile=tile, seq_rows=seq_rows),
        grid=(n_seq, n_heads, nqt),
        in_specs=[q_spec, kv_spec, kv_spec, pl.BlockSpec((tile, tile), lambda s, h, i: (0, 0))],
        out_specs=q_spec,
        out_shape=jax.ShapeDtypeStruct((n_seq * seq_rows, n_heads * SB_DH), BF16),
        scratch_shapes=[pltpu.SMEM((1,), F32)],
        compiler_params=_cparams(("parallel", "parallel", "arbitrary")),
        name="sb_prompt",
    )(qh, kh, vh, tri)


def _sb_sample_kernel(q_ref, kn_ref, vn_ref, kc_ref, vc_ref, tri_ref, o_ref, *, t_new, past, tile, n_heads):
    trir = tri_ref[...]
    tri_new = _rev_tri(t_new)
    vis_new = _strict_causal(t_new, t_new)
    for h in range(n_heads):
        sl = slice(h * SB_DH, (h + 1) * SB_DH)
        q = q_ref[:, sl]
        carry = (jnp.zeros((t_new, SB_DH), F32), jnp.zeros((t_new, 1), F32))
        acc, rest = _sb_tile(q, kn_ref[:, sl], vn_ref[:, sl], tri_new, carry, vis_new)

        def load(j, h=h):
            rows = pl.ds(pl.multiple_of(j * (tile * n_heads), tile * n_heads) + h, tile, stride=n_heads)
            return kc_ref[rows, :].astype(BF16), vc_ref[rows, :].astype(BF16)

        k_max = _key_norm_max(kc_ref, past, tile, head=h, n_heads=n_heads)
        acc = _sb_sweep(q, acc, rest, past // tile, k_max, load, trir)
        o_ref[:, sl] = acc.astype(o_ref.dtype)


def sb_sample(qh, kh, vh, cache_k, cache_v, tri, layer, *, row0, n_seq, t_new, n_heads):
    past = cache_k.shape[2] // n_heads
    tile = SB_TILE
    assert past % tile == 0 and row0 % t_new == 0
    base = row0 // t_new
    sb_w = n_heads * SB_DH
    new_spec = pl.BlockSpec((t_new, sb_w), lambda s: (base + s, 0))
    cache_spec = pl.BlockSpec((None, None, past * n_heads, SB_DH), lambda s: (layer, s, 0, 0),
                              pipeline_mode=pl.Buffered(1))
    return pl.pallas_call(
        functools.partial(_sb_sample_kernel, t_new=t_new, past=past, tile=tile, n_heads=n_heads),
        grid=(n_seq,),
        in_specs=[new_spec, new_spec, new_spec, cache_spec, cache_spec,
                  pl.BlockSpec((tile, tile), lambda s: (0, 0))],
        out_specs=pl.BlockSpec((t_new, sb_w), lambda s: (s, 0)),
        out_shape=jax.ShapeDtypeStruct((n_seq * t_new, sb_w), BF16),
        compiler_params=_cparams(("parallel",)),
        name="sb_sample",
    )(qh, kh, vh, cache_k, cache_v, tri)


def _moe_up_kernel(te_ref, tv_ref, tf_ref, x_ref, wg_ref, wu_ref, o_ref, wg_s, wu_s):
    i = pl.program_id(1)

    @pl.when(tf_ref[i] != 0)
    def _():
        wg_s[...] = wg_ref[...].astype(BF16)
        wu_s[...] = wu_ref[...].astype(BF16)

    @pl.when(tv_ref[i] != 0)
    def _():
        x = x_ref[...]
        a = _dot(x, wg_s[...])
        u = _dot(x, wu_s[...])
        o_ref[...] = (a * jax.nn.sigmoid(a) * u).astype(o_ref.dtype)

    @pl.when(tv_ref[i] == 0)
    def _():
        o_ref[...] = jnp.zeros_like(o_ref)


def _moe_down_kernel(te_ref, tv_ref, tf_ref, x_ref, w_ref, o_ref, w_s):
    i = pl.program_id(1)

    @pl.when(tf_ref[i] != 0)
    def _():
        w_s[...] = w_ref[...].astype(BF16)

    @pl.when(tv_ref[i] != 0)
    def _():
        o_ref[...] = _dot(x_ref[...], w_s[...])

    @pl.when(tv_ref[i] == 0)
    def _():
        o_ref[...] = jnp.zeros_like(o_ref)


def _moe_call(kern, x, ws, layer, tiles, tm, tn, out_dtype, name):
    n_rows, k = x.shape
    n = ws[0].shape[-1]
    wspec = pl.BlockSpec((None, None, k, tn), lambda j, i, te, tv, tf: (layer, te[i], 0, j))
    return pl.pallas_call(
        kern,
        grid_spec=pltpu.PrefetchScalarGridSpec(
            num_scalar_prefetch=3,
            grid=(n // tn, n_rows // tm),
            in_specs=[pl.BlockSpec((tm, k), lambda j, i, te, tv, tf: (i, 0))] + [wspec] * len(ws),
            out_specs=pl.BlockSpec((tm, tn), lambda j, i, te, tv, tf: (i, j)),
            scratch_shapes=[pltpu.VMEM((k, tn), BF16)] * len(ws)),
        out_shape=jax.ShapeDtypeStruct((n_rows, n), out_dtype),
        compiler_params=_cparams(("arbitrary", "arbitrary")),
        name=name,
    )(*tiles, x, *ws)


def moe_ffn(h, hn, top_idx, top_w, wg, wu, wd, layer, out_rows=None):
    m = hn.shape[0]
    n_assign = 2 * m
    tm = MOE_TILE
    n_rows = _round_up(n_assign + N_EXPERTS * (tm - 1), tm)
    n_tiles = n_rows // tm
    e_flat = top_idx.reshape(-1)
    order = jnp.argsort(e_flat, stable=True)
    e_sorted = e_flat[order]
    counts = jnp.zeros(N_EXPERTS, jnp.int32).at[e_flat].add(1)
    padded = (counts + tm - 1) // tm * tm
    start = jnp.cumsum(counts) - counts
    pend = jnp.cumsum(padded)
    pstart = pend - padded
    dest = (pstart[e_sorted] + (jnp.arange(n_assign, dtype=jnp.int32) - start[e_sorted])).astype(jnp.int32)
    row_tok = jnp.zeros(n_rows, jnp.int32).at[dest].set((order // 2).astype(jnp.int32))
    pos = jnp.zeros(n_assign, jnp.int32).at[order].set(dest).reshape(m, 2)
    tile_start = jnp.arange(n_tiles, dtype=jnp.int32) * tm
    tile_valid = (tile_start < pend[-1]).astype(jnp.int32)
    last_valid = jnp.maximum(pend[-1] // tm - 1, 0)
    tile_expert = jnp.clip(jnp.searchsorted(pend, tile_start, side="right"), 0, N_EXPERTS - 1).astype(jnp.int32)
    tile_expert = jnp.where(tile_valid != 0, tile_expert, tile_expert[last_valid])
    tile_first = jnp.concatenate([jnp.ones((1,), jnp.int32),
                                  (tile_expert[1:] != tile_expert[:-1]).astype(jnp.int32)])
    tiles = (tile_expert, tile_valid, tile_first)

    xg = hn[row_tok]
    hg = _moe_call(_moe_up_kernel, xg, (wg, wu), layer, tiles, tm, MOE_UP_TN, BF16, "moe_up")
    yb = _moe_call(_moe_down_kernel, hg, (wd,), layer, tiles, tm, MOE_DOWN_TN, F32, "moe_down")

    def combine(lo, hi):
        return h[lo:hi] + (yb[pos[lo:hi, 0]] * top_w[lo:hi, 0:1] + yb[pos[lo:hi, 1]] * top_w[lo:hi, 1:2])

    if out_rows is None:
        return combine(0, m)
    return [combine(lo, hi) for lo, hi in out_rows]


def kernel(x_prompt, x_sample, cache_k_sb, cache_v_sb, state_gla, meta_tokens, norm_mix, norm_ffn, w_in, w_alpha2, b_alpha, gla_norm, q_norm, k_norm, w_branch_a, w_branch_b, w_out, w_ff_gate, w_ff_up, w_ff_down, w_router, w_moe_gate, w_moe_up, w_moe_down):
    batch, seq, d_model = x_prompt.shape
    dec_batch, dec_seq, _ = x_sample.shape
    depth = w_in.shape[0]
    n_meta = meta_tokens.shape[0]
    past = cache_k_sb.shape[2]
    gla_rank, gla_kw = w_alpha2.shape[1], w_alpha2.shape[2]
    gla_w = w_branch_a.shape[1]
    sb_w = w_branch_b.shape[1]
    gla_heads = gla_w // GLA_DV
    sb_heads = sb_w // SB_DH
    d_ff = w_ff_gate.shape[2]
    assert gla_kw == gla_heads * GLA_DK and dec_seq % GLA_SUB == 0

    alr_col = 2 * gla_kw + 2 * gla_w
    k_col, v_col, r_col = gla_kw, 2 * gla_kw, 2 * gla_kw + gla_w
    gate_col = 3 * sb_w

    lp = n_meta + seq
    lpad = _round_up(lp, ROW_ALIGN)
    m_prompt = batch * lpad
    m_sample = dec_batch * dec_seq
    m_tot = m_prompt + m_sample
    tm = _tile(m_tot, 1024)
    tr = _tile(m_tot, 256)
    tn = 512

    pieces = []
    for b in range(batch):
        pieces += [meta_tokens.astype(F32), x_prompt[b], jnp.zeros((lpad - lp, d_model), F32)]
    h = jnp.concatenate(pieces + [x_sample.reshape(m_sample, d_model)], axis=0)
    out_rows = [(b * lpad + n_meta, b * lpad + lp) for b in range(batch)] + [(m_prompt, m_tot)]

    cache_k = cache_k_sb.reshape(depth, dec_batch, past * sb_heads, SB_DH)
    cache_v = cache_v_sb.reshape(depth, dec_batch, past * sb_heads, SB_DH)
    s0_prompt = jnp.zeros((batch, gla_heads, GLA_DV, GLA_DK), F32)

    d_ffp = _round_up(d_ff, D_FF_ALIGN)

    lpa = _round_up(lpad, SB_TILE)
    tri = jnp.tril(jnp.ones((SB_TILE, SB_TILE), BF16))

    def attn_rows(a):
        p = a[:m_prompt].reshape(batch, lpad, sb_w)
        return jnp.pad(p, ((0, 0), (0, lpa - lpad), (0, 0))).reshape(batch * lpa, sb_w)

    w_in_t = jnp.swapaxes(w_in, 1, 2)

    k_outs, v_outs, sp, ss_ = [], [], [], []
    for l in range(depth):
        w_alr_t = jnp.pad(w_in_t[l, alr_col:alr_col + gla_rank, :], ((0, LANE - gla_rank), (0, 0))).astype(BF16)
        w_a2 = jnp.pad(w_alpha2[l], ((0, LANE - gla_rank), (0, 0))).astype(BF16)

        xn = rmsnorm(h, norm_mix[l], tr)
        proj_a = matmul_nt(xn, w_in_t, tm, tn, F32, layer=l, n=alr_col, name="in_proj_gla")
        proj_b = matmul_nt(xn, w_in_t, tm, tn, F32, layer=l, row0=alr_col + gla_rank, name="in_proj_sb")
        lg = gla_log_decay(xn, w_alr_t, w_a2, b_alpha[l], tm)

        gla_cols = dict(n_heads=gla_heads, k_col=k_col, v_col=v_col, r_col=r_col)
        oa_p, s_p = gla(proj_a, lg, s0_prompt, gla_norm[l], row0=0, n_seq=batch, n_chunks=lpad // ROW_ALIGN,
                        chunk=ROW_ALIGN, n_valid=lp, **gla_cols)
        oa_s, s_s = gla(proj_a, lg, jnp.swapaxes(state_gla[l], -1, -2), gla_norm[l], row0=m_prompt,
                        n_seq=dec_batch, n_chunks=1, chunk=dec_seq, n_valid=dec_seq, **gla_cols)
        oa = jnp.concatenate([oa_p, oa_s], axis=0)

        k_out, v_out, qh, kh, vh = qk_norm(proj_b, q_norm[l], k_norm[l], 0, sb_w, tr)
        ob_p = sb_prompt(attn_rows(qh), attn_rows(kh), attn_rows(vh), tri, n_seq=batch, seq_rows=lpa,
                         n_heads=sb_heads)
        ob_p = ob_p.reshape(batch, lpa, sb_w)[:, :lpad].reshape(m_prompt, sb_w)
        ob_s = sb_sample(qh, kh, vh, cache_k, cache_v, tri, l, row0=m_prompt, n_seq=dec_batch, t_new=dec_seq,
                         n_heads=sb_heads)
        ob = jnp.concatenate([ob_p, ob_s], axis=0)

        merged = branch_merge(oa, w_branch_a, ob, w_branch_b, l, proj_b, gate_col, tm, tn)
        h = matmul(merged, w_out, tm, tn, F32, res=h, layer=l, name="out_proj")

        k_outs.append(k_out)
        v_outs.append(v_out)
        sp.append(jnp.swapaxes(s_p, -1, -2))
        ss_.append(jnp.swapaxes(s_s, -1, -2))

        i = l // 2
        if l % 2 == 0:
            hn = rmsnorm(h, norm_ffn[l], tr)
            pad_c = ((0, 0), (0, d_ffp - d_ff))
            wg = jnp.pad(w_ff_gate[i], pad_c).astype(BF16)
            wu = jnp.pad(w_ff_up[i], pad_c).astype(BF16)
            wd = jnp.pad(w_ff_down[i], ((0, d_ffp - d_ff), (0, 0))).astype(BF16)
            mid = swiglu_up(hn, wg, wu, tm, tn)
            h = matmul_acc_res(mid, wd, h, tm, _tile(d_model, 1024), _tile(d_ffp, 3072), name="ffn_down")
        else:
            hn, top_idx, top_w = rmsnorm_router(h, norm_ffn[l], w_router[i], tr)
            h = moe_ffn(h, hn, top_idx[:, :2], top_w[:, :2], w_moe_gate, w_moe_up, w_moe_down, i,
                        out_rows=out_rows if l == depth - 1 else None)

    outs = h if isinstance(h, list) else [h[lo:hi] for lo, hi in out_rows]
    y_prompt = jnp.stack(outs[:batch])
    y_sample = outs[batch].reshape(dec_batch, dec_seq, d_model)

    def prompt_rows(per_layer):
        a = jnp.stack([o[:m_prompt * sb_heads] for o in per_layer])
        return a.reshape(depth, batch, lpad, sb_heads, SB_DH)[:, :, :lp]

    def sample_rows(per_layer):
        a = jnp.stack([o[m_prompt * sb_heads:] for o in per_layer])
        return a.reshape(depth, dec_batch, dec_seq, sb_heads, SB_DH)

    return (y_prompt, y_sample, prompt_rows(k_outs), prompt_rows(v_outs), jnp.stack(sp),
            sample_rows(k_outs), sample_rows(v_outs), jnp.stack(ss_))
```

```python
import functools

import jax
import jax.numpy as jnp
from jax import lax
from jax.experimental import pallas as pl
from jax.experimental.pallas import tpu as pltpu

F32 = jnp.float32
BF16 = jnp.bfloat16

GLA_DK = 128
GLA_DV = 256
GLA_TAU = 16.0
SB_DH = 128
SB_SCALE = SB_DH ** -0.5
N_EXPERTS = 8
EPS = 1e-6

LANE = 128
SUBLANE = 8
V7X_VMEM_BYTES = 64 * 1024 * 1024
VMEM_LIMIT = V7X_VMEM_BYTES - 8 * 1024 * 1024

ROW_ALIGN = 128
GLA_SUB = 32
GLA_HEAD_GROUP = 12
SB_TILE = 256
SB_ZERO_EXP = -110.0
MOE_TILE = 512
MOE_UP_TN = 512
MOE_DOWN_TN = 1024
D_FF_ALIGN = 1024

NT_DIMS = (((1,), (1,)), ((), ()))
TN_DIMS = (((0,), (0,)), ((), ()))


def _cparams(sem):
    return pltpu.CompilerParams(dimension_semantics=sem, vmem_limit_bytes=VMEM_LIMIT)


def _round_up(x, m):
    return (x + m - 1) // m * m


def _tile(n, cap, align=LANE):
    best = None
    t = align
    while t <= min(n, cap):
        if n % t == 0:
            best = t
        t += align
    assert best is not None, (n, cap, align)
    return best


def _dot(a, b):
    return jnp.dot(a, b, preferred_element_type=F32)


def _softplus(z):
    return jnp.maximum(z, 0.0) + jnp.log1p(jnp.exp(-jnp.abs(z)))


def _rmsnorm_kernel(x_ref, g_ref, o_ref):
    x = x_ref[...]
    ms = jnp.mean(x * x, axis=-1, keepdims=True)
    o_ref[...] = (x * lax.rsqrt(ms + EPS) * g_ref[...]).astype(o_ref.dtype)


def rmsnorm(x, g, tr):
    m, d = x.shape
    return pl.pallas_call(
        _rmsnorm_kernel,
        grid=(m // tr,),
        in_specs=[pl.BlockSpec((tr, d), lambda i: (i, 0)),
                  pl.BlockSpec((1, d), lambda i: (0, 0))],
        out_specs=pl.BlockSpec((tr, d), lambda i: (i, 0)),
        out_shape=jax.ShapeDtypeStruct((m, d), BF16),
        compiler_params=_cparams(("parallel",)),
        name="rmsnorm",
    )(x, g.reshape(1, d))


def _rmsnorm_router_kernel(x_ref, g_ref, wr_ref, o_ref, idx_ref, wgt_ref):
    x = x_ref[...]
    ms = jnp.mean(x * x, axis=-1, keepdims=True)
    xn = x * lax.rsqrt(ms + EPS) * g_ref[...]
    o_ref[...] = xn.astype(o_ref.dtype)
    logits = jnp.dot(xn, wr_ref[...], preferred_element_type=F32,
                     precision=lax.Precision.HIGHEST)
    lane = lax.broadcasted_iota(jnp.int32, logits.shape, 1).astype(F32)
    neg = jnp.float32(-jnp.inf)
    l0 = jnp.where(lane < N_EXPERTS, logits, neg)
    m0 = jnp.max(l0, axis=-1, keepdims=True)
    i0 = jnp.min(jnp.where(l0 == m0, lane, float(LANE)), axis=-1, keepdims=True)
    l1 = jnp.where(lane == i0, neg, l0)
    m1 = jnp.max(l1, axis=-1, keepdims=True)
    i1 = jnp.min(jnp.where(l1 == m1, lane, float(LANE)), axis=-1, keepdims=True)
    e1 = jnp.exp(m1 - m0)
    den = 1.0 + e1
    idx_ref[...] = jnp.where(lane == 0.0, i0, i1).astype(jnp.int32)
    wgt_ref[...] = jnp.where(lane == 0.0, 1.0 / den, e1 / den)


def rmsnorm_router(x, g, w_router, tr):
    m, d = x.shape
    wr = jnp.pad(w_router.astype(F32), ((0, 0), (0, LANE - w_router.shape[1])))
    row = lambda i: (i, 0)
    return pl.pallas_call(
        _rmsnorm_router_kernel,
        grid=(m // tr,),
        in_specs=[pl.BlockSpec((tr, d), row),
                  pl.BlockSpec((1, d), lambda i: (0, 0)),
                  pl.BlockSpec((d, LANE), lambda i: (0, 0))],
        out_specs=[pl.BlockSpec((tr, d), row),
                   pl.BlockSpec((tr, LANE), row),
                   pl.BlockSpec((tr, LANE), row)],
        out_shape=[jax.ShapeDtypeStruct((m, d), BF16),
                   jax.ShapeDtypeStruct((m, LANE), jnp.int32),
                   jax.ShapeDtypeStruct((m, LANE), F32)],
        compiler_params=_cparams(("parallel",)),
        name="rmsnorm_router",
    )(x, g.reshape(1, d), wr)


def _wtile(w_ref):
    return w_ref[...].astype(BF16)


def _wspec(w, layer, tn):
    if w.ndim == 2:
        return pl.BlockSpec((w.shape[0], tn), lambda i, j: (0, j))
    return pl.BlockSpec((None, w.shape[1], tn), lambda i, j: (layer, 0, j))


def _mm_kernel(x_ref, w_ref, o_ref):
    o_ref[...] = _dot(x_ref[...], _wtile(w_ref)).astype(o_ref.dtype)


def _mm_nt_kernel(x_ref, wt_ref, o_ref):
    wt = (wt_ref[0] if len(wt_ref.shape) == 3 else wt_ref[...]).astype(BF16)
    o_ref[...] = lax.dot_general(x_ref[...], wt, NT_DIMS, preferred_element_type=F32).astype(o_ref.dtype)


def _mm_res_kernel(x_ref, w_ref, r_ref, o_ref):
    o_ref[...] = (r_ref[...] + _dot(x_ref[...], _wtile(w_ref))).astype(o_ref.dtype)


def matmul_nt(x, wt, tm, tn, out_dtype, layer=None, row0=0, n=None, name="matmul_nt"):
    m, k = x.shape
    n = wt.shape[-2] - row0 if n is None else n
    assert row0 % SUBLANE == 0 and tn % SUBLANE == 0
    row = lambda j: (row0 // SUBLANE + j * (tn // SUBLANE)) * SUBLANE
    if wt.ndim == 2:
        wspec = pl.BlockSpec((pl.Element(tn), pl.Element(k)), lambda i, j: (row(j), 0))
    else:
        wspec = pl.BlockSpec((pl.Element(1), pl.Element(tn), pl.Element(k)),
                             lambda i, j: (layer, row(j), 0))
    return pl.pallas_call(
        _mm_nt_kernel,
        grid=(m // tm, n // tn),
        in_specs=[pl.BlockSpec((tm, k), lambda i, j: (i, 0)), wspec],
        out_specs=pl.BlockSpec((tm, tn), lambda i, j: (i, j)),
        out_shape=jax.ShapeDtypeStruct((m, n), out_dtype),
        compiler_params=_cparams(("parallel", "arbitrary")),
        name=name,
    )(x, wt)


def matmul(x, w, tm, tn, out_dtype, res=None, layer=None, n=None, name="matmul"):
    m, k = x.shape
    n = w.shape[-1] if n is None else n
    in_specs = [pl.BlockSpec((tm, k), lambda i, j: (i, 0)), _wspec(w, layer, tn)]
    args = [x, w]
    kern = _mm_kernel
    if res is not None:
        in_specs.append(pl.BlockSpec((tm, tn), lambda i, j: (i, j)))
        args.append(res)
        kern = _mm_res_kernel
    return pl.pallas_call(
        kern,
        grid=(m // tm, n // tn),
        in_specs=in_specs,
        out_specs=pl.BlockSpec((tm, tn), lambda i, j: (i, j)),
        out_shape=jax.ShapeDtypeStruct((m, n), out_dtype),
        compiler_params=_cparams(("parallel", "arbitrary")),
        name=name,
    )(*args)


def _mm_acc_res_kernel(x_ref, w_ref, r_ref, o_ref, acc_ref):
    kk = pl.program_id(2)

    @pl.when(kk == 0)
    def _():
        acc_ref[...] = jnp.zeros_like(acc_ref)

    acc_ref[...] += _dot(x_ref[...], w_ref[...])

    @pl.when(kk == pl.num_programs(2) - 1)
    def _():
        o_ref[...] = r_ref[...] + acc_ref[...]


def matmul_acc_res(x, w, res, tm, tn, tk, name="matmul_acc"):
    m, k = x.shape
    n = w.shape[1]
    return pl.pallas_call(
        _mm_acc_res_kernel,
        grid=(m // tm, n // tn, k // tk),
        in_specs=[pl.BlockSpec((tm, tk), lambda i, j, kk: (i, kk)),
                  pl.BlockSpec((tk, tn), lambda i, j, kk: (kk, j)),
                  pl.BlockSpec((tm, tn), lambda i, j, kk: (i, j))],
        out_specs=pl.BlockSpec((tm, tn), lambda i, j, kk: (i, j)),
        out_shape=jax.ShapeDtypeStruct((m, n), F32),
        scratch_shapes=[pltpu.VMEM((tm, tn), F32)],
        compiler_params=_cparams(("parallel", "arbitrary", "arbitrary")),
        name=name,
    )(x, w, res)


def _swiglu_kernel(x_ref, wg_ref, wu_ref, o_ref):
    x = x_ref[...]
    a = _dot(x, wg_ref[...])
    u = _dot(x, wu_ref[...])
    o_ref[...] = (a * jax.nn.sigmoid(a) * u).astype(o_ref.dtype)


def swiglu_up(x, wg, wu, tm, tn):
    m, k = x.shape
    n = wg.shape[1]
    wspec = pl.BlockSpec((k, tn), lambda i, j: (0, j))
    return pl.pallas_call(
        _swiglu_kernel,
        grid=(m // tm, n // tn),
        in_specs=[pl.BlockSpec((tm, k), lambda i, j: (i, 0)), wspec, wspec],
        out_specs=pl.BlockSpec((tm, tn), lambda i, j: (i, j)),
        out_shape=jax.ShapeDtypeStruct((m, n), BF16),
        compiler_params=_cparams(("parallel", "arbitrary")),
        name="swiglu_up",
    )(x, wg, wu)


def _merge_kernel(oa_ref, wa_ref, ob_ref, wb_ref, ga_ref, gb_ref, o_ref):
    a = _dot(oa_ref[...], _wtile(wa_ref))
    b = _dot(ob_ref[...], _wtile(wb_ref))
    o_ref[...] = (jax.nn.sigmoid(ga_ref[...]) * a + jax.nn.sigmoid(gb_ref[...]) * b).astype(o_ref.dtype)


def branch_merge(oa, wa, ob, wb, layer, proj, gate_col, tm, tn):
    m, ka = oa.shape
    kb = ob.shape[1]
    n = wa.shape[-1]
    ga0 = gate_col // tn
    gb0 = (gate_col + n) // tn
    return pl.pallas_call(
        _merge_kernel,
        grid=(m // tm, n // tn),
        in_specs=[pl.BlockSpec((tm, ka), lambda i, j: (i, 0)),
                  _wspec(wa, layer, tn),
                  pl.BlockSpec((tm, kb), lambda i, j: (i, 0)),
                  _wspec(wb, layer, tn),
                  pl.BlockSpec((tm, tn), lambda i, j: (i, ga0 + j)),
                  pl.BlockSpec((tm, tn), lambda i, j: (i, gb0 + j))],
        out_specs=pl.BlockSpec((tm, tn), lambda i, j: (i, j)),
        out_shape=jax.ShapeDtypeStruct((m, n), BF16),
        compiler_params=_cparams(("parallel", "arbitrary")),
        name="branch_merge",
    )(oa, wa, ob, wb, proj, proj)


def _alpha_kernel(x_ref, w1t_ref, w2_ref, b_ref, o_ref):
    a = lax.dot_general(x_ref[...], _wtile(w1t_ref), NT_DIMS, preferred_element_type=F32)
    z = _dot(a.astype(BF16), w2_ref[...]) + b_ref[...]
    o_ref[...] = (jnp.minimum(z, 0.0) - jnp.log1p(jnp.exp(-jnp.abs(z)))) * (1.0 / GLA_TAU)


def gla_log_decay(xn, w1t, w2, b, tr):
    m, d = xn.shape
    kw = w2.shape[1]
    return pl.pallas_call(
        _alpha_kernel,
        grid=(m // tr,),
        in_specs=[pl.BlockSpec((tr, d), lambda i: (i, 0)),
                  pl.BlockSpec((LANE, d), lambda i: (0, 0)),
                  pl.BlockSpec((LANE, kw), lambda i: (0, 0)),
                  pl.BlockSpec((1, kw), lambda i: (0, 0))],
        out_specs=pl.BlockSpec((tr, kw), lambda i: (i, 0)),
        out_shape=jax.ShapeDtypeStruct((m, kw), F32),
        compiler_params=_cparams(("parallel",)),
        name="gla_log_decay",
    )(xn, w1t, w2, b.reshape(1, kw))


def _gla_kernel(q_ref, k_ref, v_ref, r_ref, lg_ref, s0_ref, g_ref, o_ref, sout_ref, s_scr,
                *, chunk, heads, n_valid):
    c = pl.program_id(2)

    @pl.when(c == 0)
    def _():
        s_scr[...] = s0_ref[...]

    valid = None
    if n_valid is not None:
        valid = c * chunk + lax.broadcasted_iota(jnp.int32, (chunk, 1), 0) < n_valid
    ri = lax.broadcasted_iota(jnp.int32, (chunk, chunk), 0)
    ci = lax.broadcasted_iota(jnp.int32, (chunk, chunk), 1)
    tril = (ci <= ri).astype(BF16)
    g = g_ref[...]

    for hh in range(heads):
        dk = slice(hh * GLA_DK, (hh + 1) * GLA_DK)
        dv = slice(hh * GLA_DV, (hh + 1) * GLA_DV)
        o, s_new = _gla_head(q_ref[:, dk], k_ref[:, dk], v_ref[:, dv], lg_ref[:, dk], s_scr[hh],
                             tril, valid, chunk)
        s_scr[hh] = s_new
        sout_ref[hh] = s_new
        ms = jnp.mean(o * o, axis=-1, keepdims=True)
        on = o * lax.rsqrt(ms + EPS) * g
        r = r_ref[:, dv]
        o_ref[:, dv] = (on * (r * jax.nn.sigmoid(r))).astype(o_ref.dtype)


def _gla_head(q, k, v, lg, st, tril, valid, chunk):
    q = q * (GLA_DK ** -0.5)
    if valid is not None:
        lg = jnp.where(valid, lg, 0.0)
        k = jnp.where(valid, k, 0.0)

    hi = lg.astype(BF16)
    r1 = lg - hi.astype(F32)
    mid = r1.astype(BF16)
    lo = (r1 - mid.astype(F32)).astype(BF16)
    b = _dot(tril, hi) + _dot(tril, mid) + _dot(tril, lo)

    vb = v.astype(BF16)
    o_inter = lax.dot_general((q * jnp.exp(b)).astype(BF16), st.astype(BF16), NT_DIMS,
                              preferred_element_type=F32)

    outs = []
    for i in range(chunk // GLA_SUB):
        lo_r = i * GLA_SUB
        hi_r = lo_r + GLA_SUB
        mid_r = lo_r + GLA_SUB // 2 - 1
        ref_b = b[mid_r:mid_r + 1, :]
        qs = (q[lo_r:hi_r] * jnp.exp(b[lo_r:hi_r] - ref_b)).astype(BF16)
        ks = (k[:hi_r] * jnp.exp(ref_b - b[:hi_r])).astype(BF16)
        a = lax.dot_general(qs, ks, NT_DIMS, preferred_element_type=F32)
        t_idx = lo_r + lax.broadcasted_iota(jnp.int32, (GLA_SUB, hi_r), 0)
        s_idx = lax.broadcasted_iota(jnp.int32, (GLA_SUB, hi_r), 1)
        a = jnp.where(s_idx <= t_idx, a, 0.0)
        outs.append(o_inter[lo_r:hi_r] + _dot(a.astype(BF16), vb[:hi_r]))
    o = jnp.concatenate(outs, axis=0)

    b_last = b[chunk - 1:chunk, :]
    kd = (k * jnp.exp(b_last - b)).astype(BF16)
    s_new = st * jnp.exp(b_last) + lax.dot_general(vb, kd, TN_DIMS, preferred_element_type=F32)
    return o, s_new


def gla(proj, lg, s0, g, *, row0, n_seq, n_chunks, chunk, n_valid, n_heads, k_col, v_col, r_col):
    hg = GLA_HEAD_GROUP
    wk, wv = hg * GLA_DK, hg * GLA_DV
    assert n_heads % hg == 0 and k_col % wk == 0 and v_col % wv == 0 and r_col % wv == 0
    base = row0 // chunk
    kc, vc, rc = k_col // wk, v_col // wv, r_col // wv
    rowblk = lambda s, c: base + s * n_chunks + c
    kern = functools.partial(_gla_kernel, chunk=chunk, heads=hg,
                             n_valid=None if n_valid == n_chunks * chunk else n_valid)
    state_spec = pl.BlockSpec((None, hg, GLA_DV, GLA_DK), lambda s, h, c: (s, h, 0, 0))
    return pl.pallas_call(
        kern,
        grid=(n_seq, n_heads // hg, n_chunks),
        in_specs=[pl.BlockSpec((chunk, wk), lambda s, h, c: (rowblk(s, c), h)),
                  pl.BlockSpec((chunk, wk), lambda s, h, c: (rowblk(s, c), kc + h)),
                  pl.BlockSpec((chunk, wv), lambda s, h, c: (rowblk(s, c), vc + h)),
                  pl.BlockSpec((chunk, wv), lambda s, h, c: (rowblk(s, c), rc + h)),
                  pl.BlockSpec((chunk, wk), lambda s, h, c: (rowblk(s, c), h)),
                  state_spec,
                  pl.BlockSpec((1, GLA_DV), lambda s, h, c: (0, 0))],
        out_specs=[pl.BlockSpec((chunk, wv), lambda s, h, c: (s * n_chunks + c, h)),
                   state_spec],
        out_shape=[jax.ShapeDtypeStruct((n_seq * n_chunks * chunk, n_heads * GLA_DV), BF16),
                   jax.ShapeDtypeStruct((n_seq, n_heads, GLA_DV, GLA_DK), F32)],
        scratch_shapes=[pltpu.VMEM((hg, GLA_DV, GLA_DK), F32)],
        compiler_params=_cparams(("parallel", "parallel", "arbitrary")),
        name="gla_chunk",
    )(proj, proj, proj, proj, lg, s0, g.reshape(1, GLA_DV))


def _qknorm_kernel(q_ref, k_ref, v_ref, qg_ref, kg_ref, ko_ref, vo_ref, qh_ref, kh_ref, vh_ref,
                   *, n_heads, tr):
    qg = qg_ref[...]
    kg = kg_ref[...]
    for h in range(n_heads):
        sl = slice(h * SB_DH, (h + 1) * SB_DH)
        q = q_ref[:, sl]
        k = k_ref[:, sl]
        v = v_ref[:, sl]
        qn = q * lax.rsqrt(jnp.mean(q * q, axis=-1, keepdims=True) + EPS) * qg
        kn = k * lax.rsqrt(jnp.mean(k * k, axis=-1, keepdims=True) + EPS) * kg
        ko_ref[pl.ds(h, tr, stride=n_heads), :] = kn
        vo_ref[pl.ds(h, tr, stride=n_heads), :] = v
        qh_ref[:, sl] = (qn * SB_SCALE).astype(BF16)
        kh_ref[:, sl] = kn.astype(BF16)
        vh_ref[:, sl] = v.astype(BF16)


def qk_norm(proj, qg, kg, q_col, sb_w, tr):
    m = proj.shape[0]
    n_heads = sb_w // SB_DH
    c0 = q_col // sb_w
    row = lambda i: (i, 0)
    out = jax.ShapeDtypeStruct((m, sb_w), BF16)
    cache = jax.ShapeDtypeStruct((m * n_heads, SB_DH), F32)
    return pl.pallas_call(
        functools.partial(_qknorm_kernel, n_heads=n_heads, tr=tr),
        grid=(m // tr,),
        in_specs=[pl.BlockSpec((tr, sb_w), lambda i: (i, c0)),
                  pl.BlockSpec((tr, sb_w), lambda i: (i, c0 + 1)),
                  pl.BlockSpec((tr, sb_w), lambda i: (i, c0 + 2)),
                  pl.BlockSpec((1, SB_DH), lambda i: (0, 0)),
                  pl.BlockSpec((1, SB_DH), lambda i: (0, 0))],
        out_specs=[pl.BlockSpec((tr * n_heads, SB_DH), row)] * 2 + [pl.BlockSpec((tr, sb_w), row)] * 3,
        out_shape=[cache, cache, out, out, out],
        compiler_params=_cparams(("parallel",)),
        name="qk_norm",
    )(proj, proj, proj, qg.reshape(1, SB_DH), kg.reshape(1, SB_DH))


def _sb_tile(q, kblk, vblk, trir, carry, vis):
    acc, rest = carry
    z = lax.dot_general(q, kblk, NT_DIMS, preferred_element_type=F32)
    sp = _softplus(z)
    if vis is not None:
        sp = jnp.where(vis, sp, 0.0)
    hi = sp.astype(BF16)
    lo = (sp - hi.astype(F32)).astype(BF16)
    cs = _dot(hi, trir) + _dot(lo, trir)
    w = jnp.exp(z - cs - rest)
    if vis is not None:
        w = jnp.where(vis, w, 0.0)
    acc = acc + _dot(w.astype(BF16), vblk)
    return acc, rest + cs[:, 0:1]


def _rev_tri(n):
    s = lax.broadcasted_iota(jnp.int32, (n, n), 0)
    r = lax.broadcasted_iota(jnp.int32, (n, n), 1)
    return (s >= r).astype(BF16)


def _strict_causal(tq, tk):
    t = lax.broadcasted_iota(jnp.int32, (tq, tk), 0)
    s = lax.broadcasted_iota(jnp.int32, (tq, tk), 1)
    return s < t


def _key_norm_max(k_ref, rows, step, head=0, n_heads=1):
    def body(c, m):
        base = pl.multiple_of(c * (step * n_heads), step * n_heads)
        if n_heads == 1:
            kf = k_ref[pl.ds(base, step), :].astype(F32)
        else:
            kf = k_ref[pl.ds(base + head, step, stride=n_heads), :].astype(F32)
        return jnp.maximum(m, jnp.sum(kf * kf, axis=-1, keepdims=True))

    m = lax.fori_loop(0, rows // step, body, jnp.zeros((step, 1), F32))
    return jnp.sqrt(jnp.max(m))


def _sb_sweep(q, acc, rest, n_tiles, k_norm_max, load_tile, trir):
    qf = q.astype(F32)
    q_norm_max = jnp.sqrt(jnp.max(jnp.sum(qf * qf, axis=-1, keepdims=True)))
    z_bound = q_norm_max * k_norm_max * 1.01 + 1.0

    def live(rest):
        return jnp.logical_not(z_bound - jnp.min(rest) < SB_ZERO_EXP).astype(jnp.int32)

    def cond(c):
        return jnp.logical_and(c[0] >= 0, c[3] != 0)

    def body(c):
        j, acc, rest, _ = c
        kblk, vblk = load_tile(j)
        acc, rest = _sb_tile(q, kblk, vblk, trir, (acc, rest), None)
        return j - 1, acc, rest, live(rest)

    return lax.while_loop(cond, body, (n_tiles - 1, acc, rest, live(rest)))[1]


def _sb_prompt_kernel(q_ref, k_ref, v_ref, tri_ref, o_ref, kmax_ref, *, tile, seq_rows):
    i = pl.program_id(2)

    @pl.when(i == 0)
    def _():
        kmax_ref[0] = _key_norm_max(k_ref, seq_rows, tile)

    q = q_ref[...]
    trir = tri_ref[...]

    def load(j):
        off = pl.multiple_of(j * tile, tile)
        return k_ref[pl.ds(off, tile), :], v_ref[pl.ds(off, tile), :]

    carry = (jnp.zeros((tile, SB_DH), F32), jnp.zeros((tile, 1), F32))
    kd, vd = load(i)
    acc, rest = _sb_tile(q, kd, vd, trir, carry, _strict_causal(tile, tile))
    acc = _sb_sweep(q, acc, rest, i, kmax_ref[0], load, trir)
    o_ref[...] = acc.astype(o_ref.dtype)


def sb_prompt(qh, kh, vh, tri, *, n_seq, seq_rows, n_heads):
    tile = SB_TILE
    nqt = seq_rows // tile
    kv_spec = pl.BlockSpec((seq_rows, SB_DH), lambda s, h, i: (s, h))
    q_spec = pl.BlockSpec((tile, SB_DH), lambda s, h, i: (s * nqt + i, h))
    return pl.pallas_call(
        functools.partial(_sb_prompt_kernel, tile=tile, seq_rows=seq_rows),
        grid=(n_seq, n_heads, nqt),
        in_specs=[q_spec, kv_spec, kv_spec, pl.BlockSpec((tile, tile), lambda s, h, i: (0, 0))],
        out_specs=q_spec,
        out_shape=jax.ShapeDtypeStruct((n_seq * seq_rows, n_heads * SB_DH), BF16),
        scratch_shapes=[pltpu.SMEM((1,), F32)],
        compiler_params=_cparams(("parallel", "parallel", "arbitrary")),
        name="sb_prompt",
    )(qh, kh, vh, tri)


def _sb_sample_kernel(q_ref, kn_ref, vn_ref, kc_ref, vc_ref, tri_ref, o_ref, *, t_new, past, tile, n_heads):
    trir = tri_ref[...]
    tri_new = _rev_tri(t_new)
    vis_new = _strict_causal(t_new, t_new)
    for h in range(n_heads):
        sl = slice(h * SB_DH, (h + 1) * SB_DH)
        q = q_ref[:, sl]
        carry = (jnp.zeros((t_new, SB_DH), F32), jnp.zeros((t_new, 1), F32))
        acc, rest = _sb_tile(q, kn_ref[:, sl], vn_ref[:, sl], tri_new, carry, vis_new)

        def load(j, h=h):
            rows = pl.ds(pl.multiple_of(j * (tile * n_heads), tile * n_heads) + h, tile, stride=n_heads)
            return kc_ref[rows, :].astype(BF16), vc_ref[rows, :].astype(BF16)

        k_max = _key_norm_max(kc_ref, past, tile, head=h, n_heads=n_heads)
        acc = _sb_sweep(q, acc, rest, past // tile, k_max, load, trir)
        o_ref[:, sl] = acc.astype(o_ref.dtype)


def sb_sample(qh, kh, vh, cache_k, cache_v, tri, layer, *, row0, n_seq, t_new, n_heads):
    past = cache_k.shape[2] // n_heads
    tile = SB_TILE
    assert past % tile == 0 and row0 % t_new == 0
    base = row0 // t_new
    sb_w = n_heads * SB_DH
    new_spec = pl.BlockSpec((t_new, sb_w), lambda s: (base + s, 0))
    cache_spec = pl.BlockSpec((None, None, past * n_heads, SB_DH), lambda s: (layer, s, 0, 0),
                              pipeline_mode=pl.Buffered(1))
    return pl.pallas_call(
        functools.partial(_sb_sample_kernel, t_new=t_new, past=past, tile=tile, n_heads=n_heads),
        grid=(n_seq,),
        in_specs=[new_spec, new_spec, new_spec, cache_spec, cache_spec,
                  pl.BlockSpec((tile, tile), lambda s: (0, 0))],
        out_specs=pl.BlockSpec((t_new, sb_w), lambda s: (s, 0)),
        out_shape=jax.ShapeDtypeStruct((n_seq * t_new, sb_w), BF16),
        compiler_params=_cparams(("parallel",)),
        name="sb_sample",
    )(qh, kh, vh, cache_k, cache_v, tri)


def _moe_up_kernel(te_ref, tv_ref, tf_ref, x_ref, wg_ref, wu_ref, o_ref, wg_s, wu_s):
    i = pl.program_id(1)

    @pl.when(tf_ref[i] != 0)
    def _():
        wg_s[...] = wg_ref[...].astype(BF16)
        wu_s[...] = wu_ref[...].astype(BF16)

    @pl.when(tv_ref[i] != 0)
    def _():
        x = x_ref[...]
        a = _dot(x, wg_s[...])
        u = _dot(x, wu_s[...])
        o_ref[...] = (a * jax.nn.sigmoid(a) * u).astype(o_ref.dtype)

    @pl.when(tv_ref[i] == 0)
    def _():
        o_ref[...] = jnp.zeros_like(o_ref)


def _moe_down_kernel(te_ref, tv_ref, tf_ref, x_ref, w_ref, o_ref, w_s):
    i = pl.program_id(1)

    @pl.when(tf_ref[i] != 0)
    def _():
        w_s[...] = w_ref[...].astype(BF16)

    @pl.when(tv_ref[i] != 0)
    def _():
        o_ref[...] = _dot(x_ref[...], w_s[...])

    @pl.when(tv_ref[i] == 0)
    def _():
        o_ref[...] = jnp.zeros_like(o_ref)


def _moe_call(kern, x, ws, layer, tiles, tm, tn, out_dtype, name):
    n_rows, k = x.shape
    n = ws[0].shape[-1]
    wspec = pl.BlockSpec((None, None, k, tn), lambda j, i, te, tv, tf: (layer, te[i], 0, j))
    return pl.pallas_call(
        kern,
        grid_spec=pltpu.PrefetchScalarGridSpec(
            num_scalar_prefetch=3,
            grid=(n // tn, n_rows // tm),
            in_specs=[pl.BlockSpec((tm, k), lambda j, i, te, tv, tf: (i, 0))] + [wspec] * len(ws),
            out_specs=pl.BlockSpec((tm, tn), lambda j, i, te, tv, tf: (i, j)),
            scratch_shapes=[pltpu.VMEM((k, tn), BF16)] * len(ws)),
        out_shape=jax.ShapeDtypeStruct((n_rows, n), out_dtype),
        compiler_params=_cparams(("arbitrary", "arbitrary")),
        name=name,
    )(*tiles, x, *ws)


def moe_ffn(h, hn, top_idx, top_w, wg, wu, wd, layer, out_rows=None):
    m = hn.shape[0]
    n_assign = 2 * m
    tm = MOE_TILE
    n_rows = _round_up(n_assign + N_EXPERTS * (tm - 1), tm)
    n_tiles = n_rows // tm
    e_flat = top_idx.reshape(-1)
    order = jnp.argsort(e_flat, stable=True)
    e_sorted = e_flat[order]
    counts = jnp.zeros(N_EXPERTS, jnp.int32).at[e_flat].add(1)
    padded = (counts + tm - 1) // tm * tm
    start = jnp.cumsum(counts) - counts
    pend = jnp.cumsum(padded)
    pstart = pend - padded
    dest = (pstart[e_sorted] + (jnp.arange(n_assign, dtype=jnp.int32) - start[e_sorted])).astype(jnp.int32)
    row_tok = jnp.zeros(n_rows, jnp.int32).at[dest].set((order // 2).astype(jnp.int32))
    pos = jnp.zeros(n_assign, jnp.int32).at[order].set(dest).reshape(m, 2)
    tile_start = jnp.arange(n_tiles, dtype=jnp.int32) * tm
    tile_valid = (tile_start < pend[-1]).astype(jnp.int32)
    last_valid = jnp.maximum(pend[-1] // tm - 1, 0)
    tile_expert = jnp.clip(jnp.searchsorted(pend, tile_start, side="right"), 0, N_EXPERTS - 1).astype(jnp.int32)
    tile_expert = jnp.where(tile_valid != 0, tile_expert, tile_expert[last_valid])
    tile_first = jnp.concatenate([jnp.ones((1,), jnp.int32),
                                  (tile_expert[1:] != tile_expert[:-1]).astype(jnp.int32)])
    tiles = (tile_expert, tile_valid, tile_first)

    xg = hn[row_tok]
    hg = _moe_call(_moe_up_kernel, xg, (wg, wu), layer, tiles, tm, MOE_UP_TN, BF16, "moe_up")
    yb = _moe_call(_moe_down_kernel, hg, (wd,), layer, tiles, tm, MOE_DOWN_TN, F32, "moe_down")

    def combine(lo, hi):
        return h[lo:hi] + (yb[pos[lo:hi, 0]] * top_w[lo:hi, 0:1] + yb[pos[lo:hi, 1]] * top_w[lo:hi, 1:2])

    if out_rows is None:
        return combine(0, m)
    return [combine(lo, hi) for lo, hi in out_rows]


def kernel(x_prompt, x_sample, cache_k_sb, cache_v_sb, state_gla, meta_tokens, norm_mix, norm_ffn, w_in, w_alpha2, b_alpha, gla_norm, q_norm, k_norm, w_branch_a, w_branch_b, w_out, w_ff_gate, w_ff_up, w_ff_down, w_router, w_moe_gate, w_moe_up, w_moe_down):
    batch, seq, d_model = x_prompt.shape
    dec_batch, dec_seq, _ = x_sample.shape
    depth = w_in.shape[0]
    n_meta = meta_tokens.shape[0]
    past = cache_k_sb.shape[2]
    gla_rank, gla_kw = w_alpha2.shape[1], w_alpha2.shape[2]
    gla_w = w_branch_a.shape[1]
    sb_w = w_branch_b.shape[1]
    gla_heads = gla_w // GLA_DV
    sb_heads = sb_w // SB_DH
    d_ff = w_ff_gate.shape[2]
    assert gla_kw == gla_heads * GLA_DK and dec_seq % GLA_SUB == 0

    alr_col = 2 * gla_kw + 2 * gla_w
    k_col, v_col, r_col = gla_kw, 2 * gla_kw, 2 * gla_kw + gla_w
    gate_col = 3 * sb_w

    lp = n_meta + seq
    lpad = _round_up(lp, ROW_ALIGN)
    m_prompt = batch * lpad
    m_sample = dec_batch * dec_seq
    m_tot = m_prompt + m_sample
    tm = _tile(m_tot, 1024)
    tr = _tile(m_tot, 256)
    tn = 512

    pieces = []
    for b in range(batch):
        pieces += [meta_tokens.astype(F32), x_prompt[b], jnp.zeros((lpad - lp, d_model), F32)]
    h = jnp.concatenate(pieces + [x_sample.reshape(m_sample, d_model)], axis=0)
    out_rows = [(b * lpad + n_meta, b * lpad + lp) for b in range(batch)] + [(m_prompt, m_tot)]

    cache_k = cache_k_sb.reshape(depth, dec_batch, past * sb_heads, SB_DH)
    cache_v = cache_v_sb.reshape(depth, dec_batch, past * sb_heads, SB_DH)
    s0_prompt = jnp.zeros((batch, gla_heads, GLA_DV, GLA_DK), F32)

    d_ffp = _round_up(d_ff, D_FF_ALIGN)

    lpa = _round_up(lpad, SB_TILE)
    tri = jnp.tril(jnp.ones((SB_TILE, SB_TILE), BF16))

    def attn_rows(a):
        p = a[:m_prompt].reshape(batch, lpad, sb_w)
        return jnp.pad(p, ((0, 0), (0, lpa - lpad), (0, 0))).reshape(batch * lpa, sb_w)

    w_in_t = jnp.swapaxes(w_in, 1, 2)

    k_outs, v_outs, sp, ss_ = [], [], [], []
    for l in range(depth):
        w_alr_t = jnp.pad(w_in_t[l, alr_col:alr_col + gla_rank, :], ((0, LANE - gla_rank), (0, 0)))
        w_a2 = jnp.pad(w_alpha2[l], ((0, LANE - gla_rank), (0, 0))).astype(BF16)

        xn = rmsnorm(h, norm_mix[l], tr)
        proj_a = matmul_nt(xn, w_in_t, tm, tn, F32, layer=l, n=alr_col, name="in_proj_gla")
        proj_b = matmul_nt(xn, w_in_t, tm, tn, F32, layer=l, row0=alr_col + gla_rank, name="in_proj_sb")
        lg = gla_log_decay(xn, w_alr_t, w_a2, b_alpha[l], tm)

        gla_cols = dict(n_heads=gla_heads, k_col=k_col, v_col=v_col, r_col=r_col)
        oa_p, s_p = gla(proj_a, lg, s0_prompt, gla_norm[l], row0=0, n_seq=batch, n_chunks=lpad // ROW_ALIGN,
                        chunk=ROW_ALIGN, n_valid=lp, **gla_cols)
        oa_s, s_s = gla(proj_a, lg, jnp.swapaxes(state_gla[l], -1, -2), gla_norm[l], row0=m_prompt,
                        n_seq=dec_batch, n_chunks=1, chunk=dec_seq, n_valid=dec_seq, **gla_cols)
        oa = jnp.concatenate([oa_p, oa_s], axis=0)

        k_out, v_out, qh, kh, vh = qk_norm(proj_b, q_norm[l], k_norm[l], 0, sb_w, tr)
        ob_p = sb_prompt(attn_rows(qh), attn_rows(kh), attn_rows(vh), tri, n_seq=batch, seq_rows=lpa,
                         n_heads=sb_heads)
        ob_p = ob_p.reshape(batch, lpa, sb_w)[:, :lpad].reshape(m_prompt, sb_w)
        ob_s = sb_sample(qh, kh, vh, cache_k, cache_v, tri, l, row0=m_prompt, n_seq=dec_batch, t_new=dec_seq,
                         n_heads=sb_heads)
        ob = jnp.concatenate([ob_p, ob_s], axis=0)

        merged = branch_merge(oa, w_branch_a, ob, w_branch_b, l, proj_b, gate_col, tm, tn)
        h = matmul(merged, w_out, tm, tn, F32, res=h, layer=l, name="out_proj")

        k_outs.append(k_out)
        v_outs.append(v_out)
        sp.append(jnp.swapaxes(s_p, -1, -2))
        ss_.append(jnp.swapaxes(s_s, -1, -2))

        i = l // 2
        if l % 2 == 0:
            hn = rmsnorm(h, norm_ffn[l], tr)
            pad_c = ((0, 0), (0, d_ffp - d_ff))
            wg = jnp.pad(w_ff_gate[i], pad_c).astype(BF16)
            wu = jnp.pad(w_ff_up[i], pad_c).astype(BF16)
            wd = jnp.pad(w_ff_down[i], ((0, d_ffp - d_ff), (0, 0))).astype(BF16)
            mid = swiglu_up(hn, wg, wu, tm, tn)
            h = matmul_acc_res(mid, wd, h, tm, _tile(d_model, 1024), _tile(d_ffp, 3072), name="ffn_down")
        else:
            hn, top_idx, top_w = rmsnorm_router(h, norm_ffn[l], w_router[i], tr)
            h = moe_ffn(h, hn, top_idx[:, :2], top_w[:, :2], w_moe_gate, w_moe_up, w_moe_down, i,
                        out_rows=out_rows if l == depth - 1 else None)

    outs = h if isinstance(h, list) else [h[lo:hi] for lo, hi in out_rows]
    y_prompt = jnp.stack(outs[:batch])
    y_sample = outs[batch].reshape(dec_batch, dec_seq, d_model)

    def prompt_rows(per_layer):
        a = jnp.stack([o[:m_prompt * sb_heads] for o in per_layer])
        return a.reshape(depth, batch, lpad, sb_heads, SB_DH)[:, :, :lp]

    def sample_rows(per_layer):
        a = jnp.stack([o[m_prompt * sb_heads:] for o in per_layer])
        return a.reshape(depth, dec_batch, dec_seq, sb_heads, SB_DH)

    return (y_prompt, y_sample, prompt_rows(k_outs), prompt_rows(v_outs), jnp.stack(sp),
            sample_rows(k_outs), sample_rows(v_outs), jnp.stack(ss_))
```

```python
import functools

import jax
import jax.numpy as jnp
from jax import lax
from jax.experimental import pallas as pl
from jax.experimental.pallas import tpu as pltpu

F32 = jnp.float32
BF16 = jnp.bfloat16

GLA_DK = 128
GLA_DV = 256
GLA_TAU = 16.0
SB_DH = 128
SB_SCALE = SB_DH ** -0.5
N_EXPERTS = 8
EPS = 1e-6

LANE = 128
SUBLANE = 8
V7X_VMEM_BYTES = 64 * 1024 * 1024
VMEM_LIMIT = V7X_VMEM_BYTES - 8 * 1024 * 1024

ROW_ALIGN = 128
GLA_SUB = 32
GLA_HEAD_GROUP = 12
SB_TILE = 256
SB_HEAD_GROUP = 2
SB_ZERO_EXP = -110.0
MOE_TILE = 512
MOE_UP_TN = 512
MOE_DOWN_TN = 1024
D_FF_ALIGN = 1024

NT_DIMS = (((1,), (1,)), ((), ()))
TN_DIMS = (((0,), (0,)), ((), ()))


def _cparams(sem):
    return pltpu.CompilerParams(dimension_semantics=sem, vmem_limit_bytes=VMEM_LIMIT)


def _round_up(x, m):
    return (x + m - 1) // m * m


def _tile(n, cap, align=LANE):
    best = None
    t = align
    while t <= min(n, cap):
        if n % t == 0:
            best = t
        t += align
    assert best is not None, (n, cap, align)
    return best


def _dot(a, b):
    return jnp.dot(a, b, preferred_element_type=F32)


def _softplus(z):
    return jnp.maximum(z, 0.0) + jnp.log1p(jnp.exp(-jnp.abs(z)))


def _rmsnorm_kernel(x_ref, g_ref, o_ref):
    x = x_ref[...]
    ms = jnp.mean(x * x, axis=-1, keepdims=True)
    o_ref[...] = (x * lax.rsqrt(ms + EPS) * g_ref[...]).astype(o_ref.dtype)


def rmsnorm(x, g, tr):
    m, d = x.shape
    return pl.pallas_call(
        _rmsnorm_kernel,
        grid=(m // tr,),
        in_specs=[pl.BlockSpec((tr, d), lambda i: (i, 0)),
                  pl.BlockSpec((1, d), lambda i: (0, 0))],
        out_specs=pl.BlockSpec((tr, d), lambda i: (i, 0)),
        out_shape=jax.ShapeDtypeStruct((m, d), BF16),
        compiler_params=_cparams(("parallel",)),
        name="rmsnorm",
    )(x, g.reshape(1, d))


def _rmsnorm_router_kernel(x_ref, g_ref, wr_ref, o_ref, idx_ref, wgt_ref):
    x = x_ref[...]
    ms = jnp.mean(x * x, axis=-1, keepdims=True)
    xn = x * lax.rsqrt(ms + EPS) * g_ref[...]
    o_ref[...] = xn.astype(o_ref.dtype)
    logits = jnp.dot(xn, wr_ref[...], preferred_element_type=F32,
                     precision=lax.Precision.HIGHEST)
    lane = lax.broadcasted_iota(jnp.int32, logits.shape, 1).astype(F32)
    neg = jnp.float32(-jnp.inf)
    l0 = jnp.where(lane < N_EXPERTS, logits, neg)
    m0 = jnp.max(l0, axis=-1, keepdims=True)
    i0 = jnp.min(jnp.where(l0 == m0, lane, float(LANE)), axis=-1, keepdims=True)
    l1 = jnp.where(lane == i0, neg, l0)
    m1 = jnp.max(l1, axis=-1, keepdims=True)
    i1 = jnp.min(jnp.where(l1 == m1, lane, float(LANE)), axis=-1, keepdims=True)
    e1 = jnp.exp(m1 - m0)
    den = 1.0 + e1
    idx_ref[...] = jnp.where(lane == 0.0, i0, i1).astype(jnp.int32)
    wgt_ref[...] = jnp.where(lane == 0.0, 1.0 / den, e1 / den)


def rmsnorm_router(x, g, w_router, tr):
    m, d = x.shape
    wr = jnp.pad(w_router.astype(F32), ((0, 0), (0, LANE - w_router.shape[1])))
    row = lambda i: (i, 0)
    return pl.pallas_call(
        _rmsnorm_router_kernel,
        grid=(m // tr,),
        in_specs=[pl.BlockSpec((tr, d), row),
                  pl.BlockSpec((1, d), lambda i: (0, 0)),
                  pl.BlockSpec((d, LANE), lambda i: (0, 0))],
        out_specs=[pl.BlockSpec((tr, d), row),
                   pl.BlockSpec((tr, LANE), row),
                   pl.BlockSpec((tr, LANE), row)],
        out_shape=[jax.ShapeDtypeStruct((m, d), BF16),
                   jax.ShapeDtypeStruct((m, LANE), jnp.int32),
                   jax.ShapeDtypeStruct((m, LANE), F32)],
        compiler_params=_cparams(("parallel",)),
        name="rmsnorm_router",
    )(x, g.reshape(1, d), wr)


def _wtile(w_ref):
    return w_ref[...].astype(BF16)


def _wspec(w, layer, tn):
    if w.ndim == 2:
        return pl.BlockSpec((w.shape[0], tn), lambda i, j: (0, j))
    return pl.BlockSpec((None, w.shape[1], tn), lambda i, j: (layer, 0, j))


def _mm_kernel(x_ref, w_ref, o_ref):
    o_ref[...] = _dot(x_ref[...], _wtile(w_ref)).astype(o_ref.dtype)


def _mm_nt_kernel(x_ref, wt_ref, o_ref):
    wt = (wt_ref[0] if len(wt_ref.shape) == 3 else wt_ref[...]).astype(BF16)
    o_ref[...] = lax.dot_general(x_ref[...], wt, NT_DIMS, preferred_element_type=F32).astype(o_ref.dtype)


def _mm_res_kernel(x_ref, w_ref, r_ref, o_ref):
    o_ref[...] = (r_ref[...] + _dot(x_ref[...], _wtile(w_ref))).astype(o_ref.dtype)


def matmul_nt(x, wt, tm, tn, out_dtype, layer=None, row0=0, n=None, name="matmul_nt"):
    m, k = x.shape
    n = wt.shape[-2] - row0 if n is None else n
    assert row0 % SUBLANE == 0 and tn % SUBLANE == 0
    row = lambda j: (row0 // SUBLANE + j * (tn // SUBLANE)) * SUBLANE
    if wt.ndim == 2:
        wspec = pl.BlockSpec((pl.Element(tn), pl.Element(k)), lambda i, j: (row(j), 0))
    else:
        wspec = pl.BlockSpec((pl.Element(1), pl.Element(tn), pl.Element(k)),
                             lambda i, j: (layer, row(j), 0))
    return pl.pallas_call(
        _mm_nt_kernel,
        grid=(m // tm, n // tn),
        in_specs=[pl.BlockSpec((tm, k), lambda i, j: (i, 0)), wspec],
        out_specs=pl.BlockSpec((tm, tn), lambda i, j: (i, j)),
        out_shape=jax.ShapeDtypeStruct((m, n), out_dtype),
        compiler_params=_cparams(("parallel", "arbitrary")),
        name=name,
    )(x, wt)


def matmul(x, w, tm, tn, out_dtype, res=None, layer=None, n=None, name="matmul"):
    m, k = x.shape
    n = w.shape[-1] if n is None else n
    in_specs = [pl.BlockSpec((tm, k), lambda i, j: (i, 0)), _wspec(w, layer, tn)]
    args = [x, w]
    kern = _mm_kernel
    if res is not None:
        in_specs.append(pl.BlockSpec((tm, tn), lambda i, j: (i, j)))
        args.append(res)
        kern = _mm_res_kernel
    return pl.pallas_call(
        kern,
        grid=(m // tm, n // tn),
        in_specs=in_specs,
        out_specs=pl.BlockSpec((tm, tn), lambda i, j: (i, j)),
        out_shape=jax.ShapeDtypeStruct((m, n), out_dtype),
        compiler_params=_cparams(("parallel", "arbitrary")),
        name=name,
    )(*args)


def _mm_acc_res_kernel(x_ref, w_ref, r_ref, o_ref, acc_ref):
    kk = pl.program_id(2)

    @pl.when(kk == 0)
    def _():
        acc_ref[...] = jnp.zeros_like(acc_ref)

    acc_ref[...] += _dot(x_ref[...], w_ref[...])

    @pl.when(kk == pl.num_programs(2) - 1)
    def _():
        o_ref[...] = r_ref[...] + acc_ref[...]


def matmul_acc_res(x, w, res, tm, tn, tk, name="matmul_acc"):
    m, k = x.shape
    n = w.shape[1]
    return pl.pallas_call(
        _mm_acc_res_kernel,
        grid=(m // tm, n // tn, k // tk),
        in_specs=[pl.BlockSpec((tm, tk), lambda i, j, kk: (i, kk)),
                  pl.BlockSpec((tk, tn), lambda i, j, kk: (kk, j)),
                  pl.BlockSpec((tm, tn), lambda i, j, kk: (i, j))],
        out_specs=pl.BlockSpec((tm, tn), lambda i, j, kk: (i, j)),
        out_shape=jax.ShapeDtypeStruct((m, n), F32),
        scratch_shapes=[pltpu.VMEM((tm, tn), F32)],
        compiler_params=_cparams(("parallel", "arbitrary", "arbitrary")),
        name=name,
    )(x, w, res)


def _swiglu_kernel(x_ref, wg_ref, wu_ref, o_ref):
    x = x_ref[...]
    a = _dot(x, wg_ref[...])
    u = _dot(x, wu_ref[...])
    o_ref[...] = (a * jax.nn.sigmoid(a) * u).astype(o_ref.dtype)


def swiglu_up(x, wg, wu, tm, tn):
    m, k = x.shape
    n = wg.shape[1]
    wspec = pl.BlockSpec((k, tn), lambda i, j: (0, j))
    return pl.pallas_call(
        _swiglu_kernel,
        grid=(m // tm, n // tn),
        in_specs=[pl.BlockSpec((tm, k), lambda i, j: (i, 0)), wspec, wspec],
        out_specs=pl.BlockSpec((tm, tn), lambda i, j: (i, j)),
        out_shape=jax.ShapeDtypeStruct((m, n), BF16),
        compiler_params=_cparams(("parallel", "arbitrary")),
        name="swiglu_up",
    )(x, wg, wu)


def _merge_kernel(oa_ref, wa_ref, ob_ref, wb_ref, ga_ref, gb_ref, o_ref):
    a = _dot(oa_ref[...], _wtile(wa_ref))
    b = _dot(ob_ref[...], _wtile(wb_ref))
    o_ref[...] = (jax.nn.sigmoid(ga_ref[...]) * a + jax.nn.sigmoid(gb_ref[...]) * b).astype(o_ref.dtype)


def branch_merge(oa, wa, ob, wb, layer, proj, gate_col, tm, tn):
    m, ka = oa.shape
    kb = ob.shape[1]
    n = wa.shape[-1]
    ga0 = gate_col // tn
    gb0 = (gate_col + n) // tn
    return pl.pallas_call(
        _merge_kernel,
        grid=(m // tm, n // tn),
        in_specs=[pl.BlockSpec((tm, ka), lambda i, j: (i, 0)),
                  _wspec(wa, layer, tn),
                  pl.BlockSpec((tm, kb), lambda i, j: (i, 0)),
                  _wspec(wb, layer, tn),
                  pl.BlockSpec((tm, tn), lambda i, j: (i, ga0 + j)),
                  pl.BlockSpec((tm, tn), lambda i, j: (i, gb0 + j))],
        out_specs=pl.BlockSpec((tm, tn), lambda i, j: (i, j)),
        out_shape=jax.ShapeDtypeStruct((m, n), BF16),
        compiler_params=_cparams(("parallel", "arbitrary")),
        name="branch_merge",
    )(oa, wa, ob, wb, proj, proj)


def _alpha_kernel(x_ref, w1t_ref, w2_ref, b_ref, o_ref):
    a = lax.dot_general(x_ref[...], _wtile(w1t_ref), NT_DIMS, preferred_element_type=F32)
    z = _dot(a.astype(BF16), w2_ref[...]) + b_ref[...]
    o_ref[...] = (jnp.minimum(z, 0.0) - jnp.log1p(jnp.exp(-jnp.abs(z)))) * (1.0 / GLA_TAU)


def gla_log_decay(xn, w1t, w2, b, tr):
    m, d = xn.shape
    kw = w2.shape[1]
    return pl.pallas_call(
        _alpha_kernel,
        grid=(m // tr,),
        in_specs=[pl.BlockSpec((tr, d), lambda i: (i, 0)),
                  pl.BlockSpec((LANE, d), lambda i: (0, 0)),
                  pl.BlockSpec((LANE, kw), lambda i: (0, 0)),
                  pl.BlockSpec((1, kw), lambda i: (0, 0))],
        out_specs=pl.BlockSpec((tr, kw), lambda i: (i, 0)),
        out_shape=jax.ShapeDtypeStruct((m, kw), F32),
        compiler_params=_cparams(("parallel",)),
        name="gla_log_decay",
    )(xn, w1t, w2, b.reshape(1, kw))


def _gla_kernel(q_ref, k_ref, v_ref, r_ref, lg_ref, s0_ref, g_ref, o_ref, sout_ref, s_scr,
                *, chunk, heads, n_valid):
    c = pl.program_id(2)

    @pl.when(c == 0)
    def _():
        s_scr[...] = s0_ref[...]

    valid = None
    if n_valid is not None:
        valid = c * chunk + lax.broadcasted_iota(jnp.int32, (chunk, 1), 0) < n_valid
    ri = lax.broadcasted_iota(jnp.int32, (chunk, chunk), 0)
    ci = lax.broadcasted_iota(jnp.int32, (chunk, chunk), 1)
    tril = (ci <= ri).astype(BF16)
    g = g_ref[...]

    for hh in range(heads):
        dk = slice(hh * GLA_DK, (hh + 1) * GLA_DK)
        dv = slice(hh * GLA_DV, (hh + 1) * GLA_DV)
        o, s_new = _gla_head(q_ref[:, dk], k_ref[:, dk], v_ref[:, dv], lg_ref[:, dk], s_scr[hh],
                             tril, valid, chunk)
        s_scr[hh] = s_new
        sout_ref[hh] = s_new
        ms = jnp.mean(o * o, axis=-1, keepdims=True)
        on = o * lax.rsqrt(ms + EPS) * g
        r = r_ref[:, dv]
        o_ref[:, dv] = (on * (r * jax.nn.sigmoid(r))).astype(o_ref.dtype)


def _gla_head(q, k, v, lg, st, tril, valid, chunk):
    q = q * (GLA_DK ** -0.5)
    if valid is not None:
        lg = jnp.where(valid, lg, 0.0)
        k = jnp.where(valid, k, 0.0)

    hi = lg.astype(BF16)
    r1 = lg - hi.astype(F32)
    mid = r1.astype(BF16)
    lo = (r1 - mid.astype(F32)).astype(BF16)
    b = _dot(tril, hi) + _dot(tril, mid) + _dot(tril, lo)

    vb = v.astype(BF16)
    o_inter = lax.dot_general((q * jnp.exp(b)).astype(BF16), st.astype(BF16), NT_DIMS,
                              preferred_element_type=F32)

    outs = []
    for i in range(chunk // GLA_SUB):
        lo_r = i * GLA_SUB
        hi_r = lo_r + GLA_SUB
        mid_r = lo_r + GLA_SUB // 2 - 1
        ref_b = b[mid_r:mid_r + 1, :]
        qs = (q[lo_r:hi_r] * jnp.exp(b[lo_r:hi_r] - ref_b)).astype(BF16)
        ks = (k[:hi_r] * jnp.exp(ref_b - b[:hi_r])).astype(BF16)
        a = lax.dot_general(qs, ks, NT_DIMS, preferred_element_type=F32)
        t_idx = lo_r + lax.broadcasted_iota(jnp.int32, (GLA_SUB, hi_r), 0)
        s_idx = lax.broadcasted_iota(jnp.int32, (GLA_SUB, hi_r), 1)
        a = jnp.where(s_idx <= t_idx, a, 0.0)
        outs.append(o_inter[lo_r:hi_r] + _dot(a.astype(BF16), vb[:hi_r]))
    o = jnp.concatenate(outs, axis=0)

    b_last = b[chunk - 1:chunk, :]
    kd = (k * jnp.exp(b_last - b)).astype(BF16)
    s_new = st * jnp.exp(b_last) + lax.dot_general(vb, kd, TN_DIMS, preferred_element_type=F32)
    return o, s_new


def gla(proj, lg, s0, g, *, row0, n_seq, n_chunks, chunk, n_valid, n_heads, k_col, v_col, r_col):
    hg = GLA_HEAD_GROUP
    wk, wv = hg * GLA_DK, hg * GLA_DV
    assert n_heads % hg == 0 and k_col % wk == 0 and v_col % wv == 0 and r_col % wv == 0
    base = row0 // chunk
    kc, vc, rc = k_col // wk, v_col // wv, r_col // wv
    rowblk = lambda s, c: base + s * n_chunks + c
    kern = functools.partial(_gla_kernel, chunk=chunk, heads=hg,
                             n_valid=None if n_valid == n_chunks * chunk else n_valid)
    state_spec = pl.BlockSpec((None, hg, GLA_DV, GLA_DK), lambda s, h, c: (s, h, 0, 0))
    return pl.pallas_call(
        kern,
        grid=(n_seq, n_heads // hg, n_chunks),
        in_specs=[pl.BlockSpec((chunk, wk), lambda s, h, c: (rowblk(s, c), h)),
                  pl.BlockSpec((chunk, wk), lambda s, h, c: (rowblk(s, c), kc + h)),
                  pl.BlockSpec((chunk, wv), lambda s, h, c: (rowblk(s, c), vc + h)),
                  pl.BlockSpec((chunk, wv), lambda s, h, c: (rowblk(s, c), rc + h)),
                  pl.BlockSpec((chunk, wk), lambda s, h, c: (rowblk(s, c), h)),
                  state_spec,
                  pl.BlockSpec((1, GLA_DV), lambda s, h, c: (0, 0))],
        out_specs=[pl.BlockSpec((chunk, wv), lambda s, h, c: (s * n_chunks + c, h)),
                   state_spec],
        out_shape=[jax.ShapeDtypeStruct((n_seq * n_chunks * chunk, n_heads * GLA_DV), BF16),
                   jax.ShapeDtypeStruct((n_seq, n_heads, GLA_DV, GLA_DK), F32)],
        scratch_shapes=[pltpu.VMEM((hg, GLA_DV, GLA_DK), F32)],
        compiler_params=_cparams(("parallel", "parallel", "arbitrary")),
        name="gla_chunk",
    )(proj, proj, proj, proj, lg, s0, g.reshape(1, GLA_DV))


def _qknorm_kernel(q_ref, k_ref, v_ref, qg_ref, kg_ref, ko_ref, vo_ref, qh_ref, kh_ref, vh_ref,
                   *, n_heads, tr):
    qg = qg_ref[...]
    kg = kg_ref[...]
    for h in range(n_heads):
        sl = slice(h * SB_DH, (h + 1) * SB_DH)
        q = q_ref[:, sl]
        k = k_ref[:, sl]
        v = v_ref[:, sl]
        qn = q * lax.rsqrt(jnp.mean(q * q, axis=-1, keepdims=True) + EPS) * qg
        kn = k * lax.rsqrt(jnp.mean(k * k, axis=-1, keepdims=True) + EPS) * kg
        ko_ref[pl.ds(h, tr, stride=n_heads), :] = kn
        vo_ref[pl.ds(h, tr, stride=n_heads), :] = v
        qh_ref[:, sl] = (qn * SB_SCALE).astype(BF16)
        kh_ref[:, sl] = kn.astype(BF16)
        vh_ref[:, sl] = v.astype(BF16)


def qk_norm(proj, qg, kg, q_col, sb_w, tr):
    m = proj.shape[0]
    n_heads = sb_w // SB_DH
    c0 = q_col // sb_w
    row = lambda i: (i, 0)
    out = jax.ShapeDtypeStruct((m, sb_w), BF16)
    cache = jax.ShapeDtypeStruct((m * n_heads, SB_DH), F32)
    return pl.pallas_call(
        functools.partial(_qknorm_kernel, n_heads=n_heads, tr=tr),
        grid=(m // tr,),
        in_specs=[pl.BlockSpec((tr, sb_w), lambda i: (i, c0)),
                  pl.BlockSpec((tr, sb_w), lambda i: (i, c0 + 1)),
                  pl.BlockSpec((tr, sb_w), lambda i: (i, c0 + 2)),
                  pl.BlockSpec((1, SB_DH), lambda i: (0, 0)),
                  pl.BlockSpec((1, SB_DH), lambda i: (0, 0))],
        out_specs=[pl.BlockSpec((tr * n_heads, SB_DH), row)] * 2 + [pl.BlockSpec((tr, sb_w), row)] * 3,
        out_shape=[cache, cache, out, out, out],
        compiler_params=_cparams(("parallel",)),
        name="qk_norm",
    )(proj, proj, proj, qg.reshape(1, SB_DH), kg.reshape(1, SB_DH))


def _sb_tile(q, kblk, vblk, trir, carry, vis):
    acc, rest = carry
    z = lax.dot_general(q, kblk, NT_DIMS, preferred_element_type=F32)
    sp = _softplus(z)
    if vis is not None:
        sp = jnp.where(vis, sp, 0.0)
    hi = sp.astype(BF16)
    lo = (sp - hi.astype(F32)).astype(BF16)
    cs = _dot(hi, trir) + _dot(lo, trir)
    w = jnp.exp(z - cs - rest)
    if vis is not None:
        w = jnp.where(vis, w, 0.0)
    acc = acc + _dot(w.astype(BF16), vblk)
    return acc, rest + cs[:, 0:1]


def _rev_tri(n):
    s = lax.broadcasted_iota(jnp.int32, (n, n), 0)
    r = lax.broadcasted_iota(jnp.int32, (n, n), 1)
    return (s >= r).astype(BF16)


def _strict_causal(tq, tk):
    t = lax.broadcasted_iota(jnp.int32, (tq, tk), 0)
    s = lax.broadcasted_iota(jnp.int32, (tq, tk), 1)
    return s < t


def _key_norm_max(k_ref, rows, step, head=0, n_heads=1, cols=slice(None)):
    def body(c, m):
        base = pl.multiple_of(c * (step * n_heads), step * n_heads)
        if n_heads == 1:
            kf = k_ref[pl.ds(base, step), cols].astype(F32)
        else:
            kf = k_ref[pl.ds(base + head, step, stride=n_heads), :].astype(F32)
        return jnp.maximum(m, jnp.sum(kf * kf, axis=-1, keepdims=True))

    m = lax.fori_loop(0, rows // step, body, jnp.zeros((step, 1), F32))
    return jnp.sqrt(jnp.max(m))


def _sb_sweep(qs, accs, rests, n_tiles, k_norm_maxes, load_tiles, trir):
    def z_bound(q, k_norm_max):
        qf = q.astype(F32)
        return jnp.sqrt(jnp.max(jnp.sum(qf * qf, axis=-1, keepdims=True))) * k_norm_max * 1.01 + 1.0

    bounds = [z_bound(q, km) for q, km in zip(qs, k_norm_maxes)]

    def live(rests):
        alive = [jnp.logical_not(b - jnp.min(r) < SB_ZERO_EXP) for b, r in zip(bounds, rests)]
        return functools.reduce(jnp.logical_or, alive).astype(jnp.int32)

    def cond(c):
        return jnp.logical_and(c[0] >= 0, c[3] != 0)

    def body(c):
        j, accs, rests, _ = c
        new = [_sb_tile(q, kblk, vblk, trir, (acc, rest), None)
               for q, (kblk, vblk), acc, rest in zip(qs, load_tiles(j), accs, rests)]
        accs, rests = tuple(a for a, _ in new), tuple(r for _, r in new)
        return j - 1, accs, rests, live(rests)

    return lax.while_loop(cond, body, (n_tiles - 1, tuple(accs), tuple(rests), live(rests)))[1]


def _sb_prompt_kernel(q_ref, k_ref, v_ref, tri_ref, o_ref, kmax_ref, *, tile, seq_rows, heads):
    i = pl.program_id(2)
    cols = [slice(h * SB_DH, (h + 1) * SB_DH) for h in range(heads)]

    @pl.when(i == 0)
    def _():
        for h in range(heads):
            kmax_ref[h] = _key_norm_max(k_ref, seq_rows, tile, cols=cols[h])

    trir = tri_ref[...]
    vis = _strict_causal(tile, tile)

    def load(j):
        rows = pl.ds(pl.multiple_of(j * tile, tile), tile)
        return [(k_ref[rows, c], v_ref[rows, c]) for c in cols]

    qs = [q_ref[:, c] for c in cols]
    zero = (jnp.zeros((tile, SB_DH), F32), jnp.zeros((tile, 1), F32))
    diag = [_sb_tile(q, kd, vd, trir, zero, vis) for q, (kd, vd) in zip(qs, load(i))]
    accs = _sb_sweep(qs, [a for a, _ in diag], [r for _, r in diag], i,
                     [kmax_ref[h] for h in range(heads)], load, trir)
    for c, acc in zip(cols, accs):
        o_ref[:, c] = acc.astype(o_ref.dtype)


def sb_prompt(qh, kh, vh, tri, *, n_seq, seq_rows, n_heads):
    tile = SB_TILE
    hg = SB_HEAD_GROUP
    assert n_heads % hg == 0
    nqt = seq_rows // tile
    kv_spec = pl.BlockSpec((seq_rows, hg * SB_DH), lambda s, h, i: (s, h))
    q_spec = pl.BlockSpec((tile, hg * SB_DH), lambda s, h, i: (s * nqt + i, h))
    return pl.pallas_call(
        functools.partial(_sb_prompt_kernel, tile=tile, seq_rows=seq_rows, heads=hg),
        grid=(n_seq, n_heads // hg, nqt),
        in_specs=[q_spec, kv_spec, kv_spec, pl.BlockSpec((tile, tile), lambda s, h, i: (0, 0))],
        out_specs=q_spec,
        out_shape=jax.ShapeDtypeStruct((n_seq * seq_rows, n_heads * SB_DH), BF16),
        scratch_shapes=[pltpu.SMEM((hg,), F32)],
        compiler_params=_cparams(("parallel", "parallel", "arbitrary")),
        name="sb_prompt",
    )(qh, kh, vh, tri)


def _sb_sample_kernel(q_ref, kn_ref, vn_ref, kc_ref, vc_ref, tri_ref, o_ref, *, t_new, past, tile, n_heads):
    trir = tri_ref[...]
    tri_new = _rev_tri(t_new)
    vis_new = _strict_causal(t_new, t_new)
    zero = (jnp.zeros((t_new, SB_DH), F32), jnp.zeros((t_new, 1), F32))
    for h0 in range(0, n_heads, SB_HEAD_GROUP):
        heads = range(h0, min(h0 + SB_HEAD_GROUP, n_heads))
        cols = [slice(h * SB_DH, (h + 1) * SB_DH) for h in heads]
        qs = [q_ref[:, c] for c in cols]
        new = [_sb_tile(q, kn_ref[:, c], vn_ref[:, c], tri_new, zero, vis_new) for q, c in zip(qs, cols)]

        def load(j, heads=heads):
            base = pl.multiple_of(j * (tile * n_heads), tile * n_heads)
            rows = [pl.ds(base + h, tile, stride=n_heads) for h in heads]
            return [(kc_ref[r, :].astype(BF16), vc_ref[r, :].astype(BF16)) for r in rows]

        k_maxes = [_key_norm_max(kc_ref, past, tile, head=h, n_heads=n_heads) for h in heads]
        accs = _sb_sweep(qs, [a for a, _ in new], [r for _, r in new], past // tile, k_maxes, load, trir)
        for c, acc in zip(cols, accs):
            o_ref[:, c] = acc.astype(o_ref.dtype)


def sb_sample(qh, kh, vh, cache_k, cache_v, tri, layer, *, row0, n_seq, t_new, n_heads):
    past = cache_k.shape[2] // n_heads
    tile = SB_TILE
    assert past % tile == 0 and row0 % t_new == 0
    base = row0 // t_new
    sb_w = n_heads * SB_DH
    new_spec = pl.BlockSpec((t_new, sb_w), lambda s: (base + s, 0))
    cache_spec = pl.BlockSpec((None, None, past * n_heads, SB_DH), lambda s: (layer, s, 0, 0),
                              pipeline_mode=pl.Buffered(1))
    return pl.pallas_call(
        functools.partial(_sb_sample_kernel, t_new=t_new, past=past, tile=tile, n_heads=n_heads),
        grid=(n_seq,),
        in_specs=[new_spec, new_spec, new_spec, cache_spec, cache_spec,
                  pl.BlockSpec((tile, tile), lambda s: (0, 0))],
        out_specs=pl.BlockSpec((t_new, sb_w), lambda s: (s, 0)),
        out_shape=jax.ShapeDtypeStruct((n_seq * t_new, sb_w), BF16),
        compiler_params=_cparams(("parallel",)),
        name="sb_sample",
    )(qh, kh, vh, cache_k, cache_v, tri)


def _moe_up_kernel(te_ref, tv_ref, tf_ref, x_ref, wg_ref, wu_ref, o_ref, wg_s, wu_s):
    i = pl.program_id(1)

    @pl.when(tf_ref[i] != 0)
    def _():
        wg_s[...] = wg_ref[...].astype(BF16)
        wu_s[...] = wu_ref[...].astype(BF16)

    @pl.when(tv_ref[i] != 0)
    def _():
        x = x_ref[...]
        a = _dot(x, wg_s[...])
        u = _dot(x, wu_s[...])
        o_ref[...] = (a * jax.nn.sigmoid(a) * u).astype(o_ref.dtype)

    @pl.when(tv_ref[i] == 0)
    def _():
        o_ref[...] = jnp.zeros_like(o_ref)


def _moe_down_kernel(te_ref, tv_ref, tf_ref, x_ref, w_ref, o_ref, w_s):
    i = pl.program_id(1)

    @pl.when(tf_ref[i] != 0)
    def _():
        w_s[...] = w_ref[...].astype(BF16)

    @pl.when(tv_ref[i] != 0)
    def _():
        o_ref[...] = _dot(x_ref[...], w_s[...])

    @pl.when(tv_ref[i] == 0)
    def _():
        o_ref[...] = jnp.zeros_like(o_ref)


def _moe_call(kern, x, ws, layer, tiles, tm, tn, out_dtype, name):
    n_rows, k = x.shape
    n = ws[0].shape[-1]
    wspec = pl.BlockSpec((None, None, k, tn), lambda j, i, te, tv, tf: (layer, te[i], 0, j))
    return pl.pallas_call(
        kern,
        grid_spec=pltpu.PrefetchScalarGridSpec(
            num_scalar_prefetch=3,
            grid=(n // tn, n_rows // tm),
            in_specs=[pl.BlockSpec((tm, k), lambda j, i, te, tv, tf: (i, 0))] + [wspec] * len(ws),
            out_specs=pl.BlockSpec((tm, tn), lambda j, i, te, tv, tf: (i, j)),
            scratch_shapes=[pltpu.VMEM((k, tn), BF16)] * len(ws)),
        out_shape=jax.ShapeDtypeStruct((n_rows, n), out_dtype),
        compiler_params=_cparams(("arbitrary", "arbitrary")),
        name=name,
    )(*tiles, x, *ws)


def moe_ffn(h, hn, top_idx, top_w, wg, wu, wd, layer, out_rows=None):
    m = hn.shape[0]
    n_assign = 2 * m
    tm = MOE_TILE
    n_rows = _round_up(n_assign + N_EXPERTS * (tm - 1), tm)
    n_tiles = n_rows // tm
    e_flat = top_idx.reshape(-1)
    order = jnp.argsort(e_flat, stable=True)
    e_sorted = e_flat[order]
    counts = jnp.zeros(N_EXPERTS, jnp.int32).at[e_flat].add(1)
    padded = (counts + tm - 1) // tm * tm
    start = jnp.cumsum(counts) - counts
    pend = jnp.cumsum(padded)
    pstart = pend - padded
    dest = (pstart[e_sorted] + (jnp.arange(n_assign, dtype=jnp.int32) - start[e_sorted])).astype(jnp.int32)
    row_tok = jnp.zeros(n_rows, jnp.int32).at[dest].set((order // 2).astype(jnp.int32))
    pos = jnp.zeros(n_assign, jnp.int32).at[order].set(dest).reshape(m, 2)
    tile_start = jnp.arange(n_tiles, dtype=jnp.int32) * tm
    tile_valid = (tile_start < pend[-1]).astype(jnp.int32)
    last_valid = jnp.maximum(pend[-1] // tm - 1, 0)
    tile_expert = jnp.clip(jnp.searchsorted(pend, tile_start, side="right"), 0, N_EXPERTS - 1).astype(jnp.int32)
    tile_expert = jnp.where(tile_valid != 0, tile_expert, tile_expert[last_valid])
    tile_first = jnp.concatenate([jnp.ones((1,), jnp.int32),
                                  (tile_expert[1:] != tile_expert[:-1]).astype(jnp.int32)])
    tiles = (tile_expert, tile_valid, tile_first)

    xg = hn[row_tok]
    hg = _moe_call(_moe_up_kernel, xg, (wg, wu), layer, tiles, tm, MOE_UP_TN, BF16, "moe_up")
    yb = _moe_call(_moe_down_kernel, hg, (wd,), layer, tiles, tm, MOE_DOWN_TN, F32, "moe_down")

    def combine(lo, hi):
        return h[lo:hi] + (yb[pos[lo:hi, 0]] * top_w[lo:hi, 0:1] + yb[pos[lo:hi, 1]] * top_w[lo:hi, 1:2])

    if out_rows is None:
        return combine(0, m)
    return [combine(lo, hi) for lo, hi in out_rows]


def kernel(x_prompt, x_sample, cache_k_sb, cache_v_sb, state_gla, meta_tokens, norm_mix, norm_ffn, w_in, w_alpha2, b_alpha, gla_norm, q_norm, k_norm, w_branch_a, w_branch_b, w_out, w_ff_gate, w_ff_up, w_ff_down, w_router, w_moe_gate, w_moe_up, w_moe_down):
    batch, seq, d_model = x_prompt.shape
    dec_batch, dec_seq, _ = x_sample.shape
    depth = w_in.shape[0]
    n_meta = meta_tokens.shape[0]
    past = cache_k_sb.shape[2]
    gla_rank, gla_kw = w_alpha2.shape[1], w_alpha2.shape[2]
    gla_w = w_branch_a.shape[1]
    sb_w = w_branch_b.shape[1]
    gla_heads = gla_w // GLA_DV
    sb_heads = sb_w // SB_DH
    d_ff = w_ff_gate.shape[2]
    assert gla_kw == gla_heads * GLA_DK and dec_seq % GLA_SUB == 0

    alr_col = 2 * gla_kw + 2 * gla_w
    k_col, v_col, r_col = gla_kw, 2 * gla_kw, 2 * gla_kw + gla_w
    gate_col = 3 * sb_w

    lp = n_meta + seq
    lpad = _round_up(lp, ROW_ALIGN)
    m_prompt = batch * lpad
    m_sample = dec_batch * dec_seq
    m_tot = m_prompt + m_sample
    tm = _tile(m_tot, 1024)
    tr = _tile(m_tot, 256)
    tn = 512

    pieces = []
    for b in range(batch):
        pieces += [meta_tokens.astype(F32), x_prompt[b], jnp.zeros((lpad - lp, d_model), F32)]
    h = jnp.concatenate(pieces + [x_sample.reshape(m_sample, d_model)], axis=0)
    out_rows = [(b * lpad + n_meta, b * lpad + lp) for b in range(batch)] + [(m_prompt, m_tot)]

    cache_k = cache_k_sb.reshape(depth, dec_batch, past * sb_heads, SB_DH)
    cache_v = cache_v_sb.reshape(depth, dec_batch, past * sb_heads, SB_DH)
    s0_prompt = jnp.zeros((batch, gla_heads, GLA_DV, GLA_DK), F32)

    d_ffp = _round_up(d_ff, D_FF_ALIGN)

    lpa = _round_up(lpad, SB_TILE)
    tri = jnp.tril(jnp.ones((SB_TILE, SB_TILE), BF16))

    def attn_rows(a):
        p = a[:m_prompt].reshape(batch, lpad, sb_w)
        return jnp.pad(p, ((0, 0), (0, lpa - lpad), (0, 0))).reshape(batch * lpa, sb_w)

    w_in_t = jnp.swapaxes(w_in, 1, 2)

    k_outs, v_outs, sp, ss_ = [], [], [], []
    for l in range(depth):
        w_alr_t = jnp.pad(w_in_t[l, alr_col:alr_col + gla_rank, :], ((0, LANE - gla_rank), (0, 0)))
        w_a2 = jnp.pad(w_alpha2[l], ((0, LANE - gla_rank), (0, 0))).astype(BF16)

        xn = rmsnorm(h, norm_mix[l], tr)
        proj_a = matmul_nt(xn, w_in_t, tm, tn, F32, layer=l, n=alr_col, name="in_proj_gla")
        proj_b = matmul_nt(xn, w_in_t, tm, tn, F32, layer=l, row0=alr_col + gla_rank, name="in_proj_sb")
        lg = gla_log_decay(xn, w_alr_t, w_a2, b_alpha[l], tm)

        gla_cols = dict(n_heads=gla_heads, k_col=k_col, v_col=v_col, r_col=r_col)
        oa_p, s_p = gla(proj_a, lg, s0_prompt, gla_norm[l], row0=0, n_seq=batch, n_chunks=lpad // ROW_ALIGN,
                        chunk=ROW_ALIGN, n_valid=lp, **gla_cols)
        oa_s, s_s = gla(proj_a, lg, jnp.swapaxes(state_gla[l], -1, -2), gla_norm[l], row0=m_prompt,
                        n_seq=dec_batch, n_chunks=1, chunk=dec_seq, n_valid=dec_seq, **gla_cols)
        oa = jnp.concatenate([oa_p, oa_s], axis=0)

        k_out, v_out, qh, kh, vh = qk_norm(proj_b, q_norm[l], k_norm[l], 0, sb_w, tr)
        ob_p = sb_prompt(attn_rows(qh), attn_rows(kh), attn_rows(vh), tri, n_seq=batch, seq_rows=lpa,
                         n_heads=sb_heads)
        ob_p = ob_p.reshape(batch, lpa, sb_w)[:, :lpad].reshape(m_prompt, sb_w)
        ob_s = sb_sample(qh, kh, vh, cache_k, cache_v, tri, l, row0=m_prompt, n_seq=dec_batch, t_new=dec_seq,
                         n_heads=sb_heads)
        ob = jnp.concatenate([ob_p, ob_s], axis=0)

        merged = branch_merge(oa, w_branch_a, ob, w_branch_b, l, proj_b, gate_col, tm, tn)
        h = matmul(merged, w_out, tm, tn, F32, res=h, layer=l, name="out_proj")

        k_outs.append(k_out)
        v_outs.append(v_out)
        sp.append(jnp.swapaxes(s_p, -1, -2))
        ss_.append(jnp.swapaxes(s_s, -1, -2))

        i = l // 2
        if l % 2 == 0:
            hn = rmsnorm(h, norm_ffn[l], tr)
            pad_c = ((0, 0), (0, d_ffp - d_ff))
            wg = jnp.pad(w_ff_gate[i], pad_c).astype(BF16)
            wu = jnp.pad(w_ff_up[i], pad_c).astype(BF16)
            wd = jnp.pad(w_ff_down[i], ((0, d_ffp - d_ff), (0, 0))).astype(BF16)
            mid = swiglu_up(hn, wg, wu, tm, tn)
            h = matmul_acc_res(mid, wd, h, tm, _tile(d_model, 1024), _tile(d_ffp, 3072), name="ffn_down")
        else:
            hn, top_idx, top_w = rmsnorm_router(h, norm_ffn[l], w_router[i], tr)
            h = moe_ffn(h, hn, top_idx[:, :2], top_w[:, :2], w_moe_gate, w_moe_up, w_moe_down, i,
                        out_rows=out_rows if l == depth - 1 else None)

    outs = h if isinstance(h, list) else [h[lo:hi] for lo, hi in out_rows]
    y_prompt = jnp.stack(outs[:batch])
    y_sample = outs[batch].reshape(dec_batch, dec_seq, d_model)

    def prompt_rows(per_layer):
        a = jnp.stack([o[:m_prompt * sb_heads] for o in per_layer])
        return a.reshape(depth, batch, lpad, sb_heads, SB_DH)[:, :, :lp]

    def sample_rows(per_layer):
        a = jnp.stack([o[m_prompt * sb_heads:] for o in per_layer])
        return a.reshape(depth, dec_batch, dec_seq, sb_heads, SB_DH)

    return (y_prompt, y_sample, prompt_rows(k_outs), prompt_rows(v_outs), jnp.stack(sp),
            sample_rows(k_outs), sample_rows(v_outs), jnp.stack(ss_))
```
